```python
import jax, jax.numpy as jnp
from jax import lax
import numpy as np

D_MODEL = 1024
BATCH = 4
SEQ = 4096
DEPTH = 1
DEC_BATCH = 32
DEC_SEQ = 1
PAST_LEN = 16384
PAGE_SIZE = 128

N_HEADS = 16
HEAD_DIM = 64
ATT_W = N_HEADS * HEAD_DIM
D_RNN = D_MODEL
N_RNN_BLOCKS = 8
RNN_BLOCK = D_RNN // N_RNN_BLOCKS
RNN_CONV_W = 4
LRU_C = 8.0
D_FF = 3 * D_MODEL
FFN_CONV_W = 3
D_PLE = 256
Q_BLOCK = 128
LN_EPS = 1e-5
SB_BIAS_LO = -10.0
SB_BIAS_HI = -4.0
DEEPNORM_ALPHA = (2.0 * DEPTH) ** 0.25
DEEPNORM_BETA = (8.0 * DEPTH) ** -0.25
IN_SPLITS = (ATT_W, ATT_W, ATT_W, D_RNN, D_RNN, D_MODEL, D_MODEL)
D_IN = sum(IN_SPLITS)

kernel_name = "stick_breaking_rglru_hybrid_step"


def _layer_norm(x, g, b):
    xf = x.astype(jnp.float32)
    mu = xf.mean(-1, keepdims=True)
    var = jnp.square(xf - mu).mean(-1, keepdims=True)
    return ((xf - mu) * lax.rsqrt(var + LN_EPS) * g.astype(jnp.float32) + b.astype(jnp.float32)).astype(x.dtype)


def _causal_dwconv(x_ext, w, b):
    width = w.shape[0]
    s = x_ext.shape[1] - width + 1
    out = b
    for j in range(width):
        out = out + x_ext[:, j:j + s] * w[j]
    return out


def _sb_block(q, k, v, bias, log_rem, valid):
    z = (jnp.einsum('bqhd,bkhd->bhqk', q, k).astype(jnp.float32) * (HEAD_DIM ** -0.5)
         + bias.astype(jnp.float32)[None, :, None, None])
    log_beta = jax.nn.log_sigmoid(z)
    log_1m = jax.nn.log_sigmoid(-z)
    if valid is not None:
        log_1m = jnp.where(valid, log_1m, 0.0)
    after = lax.cumsum(log_1m, axis=3, reverse=True) - log_1m
    a = jnp.exp(log_beta + after + log_rem[..., None])
    if valid is not None:
        a = jnp.where(valid, a, 0.0)
    out = jnp.einsum('bhqk,bkhd->bqhd', a, v.astype(jnp.float32))
    return out, log_rem + log_1m.sum(-1)


def _stick_breaking_prompt(q, k, v, bias):
    bsz, s = q.shape[:2]
    outs = []
    for start in range(0, s, Q_BLOCK):
        end = min(start + Q_BLOCK, s)
        valid = jnp.arange(end)[None, :] < jnp.arange(start, end)[:, None]
        rem0 = jnp.zeros((bsz, N_HEADS, end - start), jnp.float32)
        o, _ = _sb_block(q[:, start:end], k[:, :end], v[:, :end], bias, rem0, valid)
        outs.append(o)
    return jnp.concatenate(outs, axis=1)


def _stick_breaking_sample(q, k_new, v_new, bias, k_pool, v_pool, page_table):
    bsz, nq = q.shape[:2]
    valid = jnp.arange(nq)[None, :] < jnp.arange(nq)[:, None]
    out, rem = _sb_block(q, k_new, v_new, bias, jnp.zeros((bsz, N_HEADS, nq), jnp.float32), valid)

    def step(carry, pages):
        acc, log_rem = carry
        o, log_rem = _sb_block(q, k_pool[pages], v_pool[pages], bias, log_rem, None)
        return (acc + o, log_rem), None

    (out, _), _ = lax.scan(step, (out, rem), page_table.T, reverse=True)
    return out


def _rg_lru(xc, h0, w_ra, b_ra, w_ri, b_ri, lru_lambda):
    bsz, s, _ = xc.shape
    xb = xc.reshape(bsz, s, N_RNN_BLOCKS, RNN_BLOCK)
    r = jax.nn.sigmoid(jnp.einsum('bsnc,ncd->bsnd', xb, w_ra).reshape(bsz, s, D_RNN) + b_ra)
    i = jax.nn.sigmoid(jnp.einsum('bsnc,ncd->bsnd', xb, w_ri).reshape(bsz, s, D_RNN) + b_ri)
    log_a = (-LRU_C * r.astype(jnp.float32)) * jax.nn.softplus(-lru_lambda.astype(jnp.float32))
    a = jnp.exp(log_a)
    u = (xc * i).astype(jnp.float32) * jnp.sqrt(-jnp.expm1(2.0 * log_a))
    u = u.at[:, 0].add(a[:, 0] * h0.astype(jnp.float32))

    def combine(lhs, rhs):
        a1, b1 = lhs
        a2, b2 = rhs
        return a1 * a2, a2 * b1 + b2

    _, h = lax.associative_scan(combine, (a, u), axis=1)
    return h


def _layer(x, pe, attn_fn, rconv_hist, h0, fconv_hist,
           w_in, sb_bias, w_rconv, b_rconv, w_ra, b_ra, w_ri, b_ri, lru_lambda,
           w_pa, w_pb, w_o, ln1_g, ln1_b, w_up, w_fconv, b_fconv, w_down, ln2_g, ln2_b,
           w_ple, w_pleg, b_pleg, ln3_g, ln3_b):
    bsz, s, _ = x.shape
    offs = [int(o) for o in np.cumsum(IN_SPLITS)[:-1]]
    q, k, v, xr, gr, ga, gb = jnp.split(x @ w_in, offs, axis=-1)
    q = q.reshape(bsz, s, N_HEADS, HEAD_DIM)
    k = k.reshape(bsz, s, N_HEADS, HEAD_DIM)
    v = v.reshape(bsz, s, N_HEADS, HEAD_DIM)
    att = attn_fn(q, k, v, sb_bias).reshape(bsz, s, ATT_W).astype(x.dtype)
    xr_ext = jnp.concatenate([rconv_hist.astype(xr.dtype), xr], axis=1)
    xc = _causal_dwconv(xr_ext, w_rconv, b_rconv)
    h = _rg_lru(xc, h0, w_ra, b_ra, w_ri, b_ri, lru_lambda)
    rnn = (jax.nn.gelu(gr).astype(jnp.float32) * h).astype(x.dtype)
    merged = jax.nn.sigmoid(ga) * (att @ w_pa) + jax.nn.sigmoid(gb) * (rnn @ w_pb)
    x1 = _layer_norm(DEEPNORM_ALPHA * x + merged @ w_o, ln1_g, ln1_b)
    up, gate = jnp.split(x1 @ w_up, [D_FF], axis=-1)
    gate_ext = jnp.concatenate([fconv_hist.astype(gate.dtype), gate], axis=1)
    gate_c = _causal_dwconv(gate_ext, w_fconv, b_fconv)
    x2 = _layer_norm(DEEPNORM_ALPHA * x1 + (jax.nn.gelu(gate_c) * up) @ w_down, ln2_g, ln2_b)
    e = (pe @ w_ple) * jax.nn.sigmoid(x2 @ w_pleg + b_pleg)
    x3 = _layer_norm(DEEPNORM_ALPHA * x2 + e, ln3_g, ln3_b)
    return (x3, k, v, xr_ext[:, -(RNN_CONV_W - 1):], h[:, -1].astype(h0.dtype),
            gate_ext[:, -(FFN_CONV_W - 1):])


def setup_inputs(seed: int = 0) -> dict:
    key = jax.random.key(seed)
    ks = iter(jax.random.split(key, 40))
    f32 = jnp.float32
    nrm = lambda shape, scale=1.0: jax.random.normal(next(ks), shape, f32) * scale
    n_pages = PAST_LEN // PAGE_SIZE
    n_used = DEC_BATCH * n_pages
    n_phys = n_used + n_used // 4
    u = jax.random.uniform(next(ks), (DEPTH, D_RNN), f32, 0.9, 0.999) ** (1.0 / LRU_C)
    sb_bias0 = jnp.broadcast_to(jnp.linspace(SB_BIAS_LO, SB_BIAS_HI, N_HEADS, dtype=f32), (DEPTH, N_HEADS))
    return {
        "x_prompt": nrm((BATCH, SEQ, D_MODEL)),
        "x_sample": nrm((DEC_BATCH, DEC_SEQ, D_MODEL)),
        "p_prompt": nrm((DEPTH, BATCH, SEQ, D_PLE)),
        "p_sample": nrm((DEPTH, DEC_BATCH, DEC_SEQ, D_PLE)),
        "cache_k": nrm((DEPTH, n_phys, PAGE_SIZE, N_HEADS, HEAD_DIM)),
        "cache_v": nrm((DEPTH, n_phys, PAGE_SIZE, N_HEADS, HEAD_DIM)),
        "page_table": jax.random.permutation(next(ks), n_phys)[:n_used].reshape(DEC_BATCH, n_pages).astype(jnp.int32),
        "state_rnn_conv": nrm((DEPTH, DEC_BATCH, RNN_CONV_W - 1, D_RNN)),
        "state_rnn_h": nrm((DEPTH, DEC_BATCH, D_RNN), 0.5),
        "state_ffn_conv": nrm((DEPTH, DEC_BATCH, FFN_CONV_W - 1, D_FF)),
        "w_in": nrm((DEPTH, D_MODEL, D_IN), D_MODEL ** -0.5),
        "sb_bias": sb_bias0 + nrm((DEPTH, N_HEADS), 0.01),
        "w_rconv": nrm((DEPTH, RNN_CONV_W, D_RNN), RNN_CONV_W ** -0.5),
        "b_rconv": nrm((DEPTH, D_RNN), 0.01),
        "w_ra": nrm((DEPTH, N_RNN_BLOCKS, RNN_BLOCK, RNN_BLOCK), RNN_BLOCK ** -0.5),
        "b_ra": nrm((DEPTH, D_RNN), 0.01),
        "w_ri": nrm((DEPTH, N_RNN_BLOCKS, RNN_BLOCK, RNN_BLOCK), RNN_BLOCK ** -0.5),
        "b_ri": nrm((DEPTH, D_RNN), 0.01),
        "lru_lambda": jnp.log(u) - jnp.log1p(-u),
        "w_pa": nrm((DEPTH, ATT_W, D_MODEL), ATT_W ** -0.5),
        "w_pb": nrm((DEPTH, D_RNN, D_MODEL), D_RNN ** -0.5),
        "w_o": nrm((DEPTH, D_MODEL, D_MODEL), D_MODEL ** -0.5 * DEEPNORM_BETA),
        "ln1_g": 1.0 + nrm((DEPTH, D_MODEL), 0.02),
        "ln1_b": nrm((DEPTH, D_MODEL), 0.01),
        "w_up": nrm((DEPTH, D_MODEL, 2 * D_FF), D_MODEL ** -0.5),
        "w_fconv": nrm((DEPTH, FFN_CONV_W, D_FF), FFN_CONV_W ** -0.5),
        "b_fconv": nrm((DEPTH, D_FF), 0.01),
        "w_down": nrm((DEPTH, D_FF, D_MODEL), D_FF ** -0.5 * DEEPNORM_BETA),
        "ln2_g": 1.0 + nrm((DEPTH, D_MODEL), 0.02),
        "ln2_b": nrm((DEPTH, D_MODEL), 0.01),
        "w_ple": nrm((DEPTH, D_PLE, D_MODEL), D_PLE ** -0.5 * DEEPNORM_BETA),
        "w_pleg": nrm((DEPTH, D_MODEL, D_MODEL), D_MODEL ** -0.5),
        "b_pleg": nrm((DEPTH, D_MODEL), 0.01),
        "ln3_g": 1.0 + nrm((DEPTH, D_MODEL), 0.02),
        "ln3_b": nrm((DEPTH, D_MODEL), 0.01),
    }


def reference(x_prompt, x_sample, p_prompt, p_sample, cache_k, cache_v, page_table,
              state_rnn_conv, state_rnn_h, state_ffn_conv,
              w_in, sb_bias, w_rconv, b_rconv, w_ra, b_ra, w_ri, b_ri, lru_lambda,
              w_pa, w_pb, w_o, ln1_g, ln1_b, w_up, w_fconv, b_fconv, w_down, ln2_g, ln2_b,
              w_ple, w_pleg, b_pleg, ln3_g, ln3_b):
    xp, xs = x_prompt, x_sample
    kp_l, vp_l, rcp_l, hp_l, fcp_l = [], [], [], [], []
    ks_l, vs_l, rcs_l, hs_l, fcs_l = [], [], [], [], []
    for i in range(DEPTH):
        lw = [a[i] for a in (w_in, sb_bias, w_rconv, b_rconv, w_ra, b_ra, w_ri, b_ri, lru_lambda,
                             w_pa, w_pb, w_o, ln1_g, ln1_b, w_up, w_fconv, b_fconv, w_down,
                             ln2_g, ln2_b, w_ple, w_pleg, b_pleg, ln3_g, ln3_b)]
        bp = xp.shape[0]
        xp, kp, vp, rcp, hp, fcp = _layer(
            xp, p_prompt[i], _stick_breaking_prompt,
            jnp.zeros((bp, RNN_CONV_W - 1, D_RNN), xp.dtype),
            jnp.zeros((bp, D_RNN), xp.dtype),
            jnp.zeros((bp, FFN_CONV_W - 1, D_FF), xp.dtype), *lw)
        k_pool, v_pool = cache_k[i], cache_v[i]
        attn_s = lambda q, k, v, b: _stick_breaking_sample(q, k, v, b, k_pool, v_pool, page_table)
        xs, ks_, vs_, rcs, hs, fcs = _layer(
            xs, p_sample[i], attn_s, state_rnn_conv[i], state_rnn_h[i], state_ffn_conv[i], *lw)
        kp_l.append(kp); vp_l.append(vp); rcp_l.append(rcp); hp_l.append(hp); fcp_l.append(fcp)
        ks_l.append(ks_); vs_l.append(vs_); rcs_l.append(rcs); hs_l.append(hs); fcs_l.append(fcs)
    return (xp, xs,
            jnp.stack(kp_l), jnp.stack(vp_l), jnp.stack(rcp_l), jnp.stack(hp_l), jnp.stack(fcp_l),
            jnp.stack(ks_l), jnp.stack(vs_l), jnp.stack(rcs_l), jnp.stack(hs_l), jnp.stack(fcs_l))
```

```python
import functools

import jax
import jax.numpy as jnp
from jax import lax
from jax.experimental import pallas as pl
from jax.experimental.pallas import tpu as pltpu

F32 = jnp.float32
BF16 = jnp.bfloat16

D_MODEL = 1024
N_HEADS = 16
HEAD_DIM = 64
N_RNN_BLOCKS = 8
RNN_BLOCK = D_MODEL // N_RNN_BLOCKS
RNN_CONV_W = 4
LRU_C = 8.0
D_FF = 3 * D_MODEL
FFN_CONV_W = 3
D_PLE = 256
PAGE_SIZE = 128
LN_EPS = 1e-5
DEPTH = 1
DEEPNORM_ALPHA = (2.0 * DEPTH) ** 0.25
N_IN_SPLITS = 7

LANES = 128
SUBLANES = 8
HEADS_PER_LANE_TILE = LANES // HEAD_DIM
KEY_BLOCK = 128
VMEM_LIMIT_BYTES = 56 * 1024 * 1024


def _compiler_params(semantics):
    return pltpu.CompilerParams(dimension_semantics=semantics, vmem_limit_bytes=VMEM_LIMIT_BYTES)


def _layer_norm(x, g, b):
    mu = jnp.mean(x, axis=-1, keepdims=True)
    xc = x - mu
    var = jnp.mean(xc * xc, axis=-1, keepdims=True)
    return xc * lax.rsqrt(var + LN_EPS) * g + b


def _softplus(z):
    return jnp.maximum(z, 0.0) + jnp.log(1.0 + jnp.exp(-jnp.abs(z)))


def _neg_suffix_sum_matrix():
    j = lax.broadcasted_iota(jnp.int32, (KEY_BLOCK, 2 * KEY_BLOCK), 0)
    s = lax.broadcasted_iota(jnp.int32, (KEY_BLOCK, 2 * KEY_BLOCK), 1)
    return jnp.where((j >= s) | (s >= KEY_BLOCK), -1.0, 0.0).astype(BF16)


def _inproj_kernel(*refs, tm, key_major):
    if key_major:
        x_ref, w_ref, wt_ref, q_ref, kf_ref, vf_ref, kb_ref, vb_ref, rest_ref, xb_ref = refs
        kv_refs = ((kf_ref, kb_ref), (vf_ref, vb_ref))
    else:
        x_ref, w_ref, q_ref, kf_ref, vf_ref, rest_ref, xb_ref = refs
        kv_refs = ((kf_ref, None), (vf_ref, None))
    j = pl.program_id(1)

    @pl.when(j == 0)
    def _():
        xb_ref[...] = x_ref[...].astype(BF16)

    def project():
        return jnp.dot(xb_ref[...], w_ref[...], preferred_element_type=F32)

    @pl.when(j == 0)
    def _():
        q_ref[...] = (project() * (HEAD_DIM ** -0.5)).astype(BF16)

    for split, (f_ref, b_ref) in zip((1, 2), kv_refs):
        @pl.when(j == split)
        def _(f_ref=f_ref, b_ref=b_ref):
            if key_major:
                yt = lax.dot_general(wt_ref[...], xb_ref[...], (((1,), (1,)), ((), ())),
                                     preferred_element_type=F32)
                f_ref[...] = yt
                for kb in range(tm // KEY_BLOCK):
                    b_ref[kb] = yt[:, kb * KEY_BLOCK:(kb + 1) * KEY_BLOCK].astype(BF16)
            else:
                f_ref[...] = project()

    @pl.when(j >= 3)
    def _():
        rest_ref[...] = project()


def _inproj(x, w_in_bf, tm, wt_kv_bf=None, rows_per_seq=None):
    t = x.shape[0]
    d = D_MODEL
    key_major = wt_kv_bf is not None
    x_spec = pl.BlockSpec((tm, d), lambda i, j: (i, 0))
    q_spec = pl.BlockSpec((tm, d), lambda i, j: (i, 0))
    rest_spec = pl.BlockSpec((tm, d), lambda i, j: (i, jnp.maximum(j - 3, 0)))
    q_shape = jax.ShapeDtypeStruct((t, d), BF16)
    rest_shape = jax.ShapeDtypeStruct((t, 4 * d), F32)
    if key_major:
        tiles = rows_per_seq // tm
        bsz = t // rows_per_seq
        kb_per_tile = tm // KEY_BLOCK
        in_specs = [
            x_spec,
            pl.BlockSpec((d, d), lambda i, j: (0, jnp.where((j == 1) | (j == 2), 0, j))),
            pl.BlockSpec((d, d), lambda i, j: (jnp.clip(j - 1, 0, 1), 0)),
        ]
        f_spec = pl.BlockSpec((None, d, tm), lambda i, j: (i // tiles, 0, i % tiles))
        b_spec = pl.BlockSpec((None, kb_per_tile, d, KEY_BLOCK), lambda i, j: (i // tiles, i % tiles, 0, 0))
        f_shape = jax.ShapeDtypeStruct((bsz, d, rows_per_seq), F32)
        b_shape = jax.ShapeDtypeStruct((bsz, rows_per_seq // KEY_BLOCK, d, KEY_BLOCK), BF16)
        out_specs = [q_spec, f_spec, f_spec, b_spec, b_spec, rest_spec]
        out_shape = [q_shape, f_shape, f_shape, b_shape, b_shape, rest_shape]
        args = (x, w_in_bf, wt_kv_bf)
    else:
        in_specs = [x_spec, pl.BlockSpec((d, d), lambda i, j: (0, j))]
        f_spec = pl.BlockSpec((tm, d), lambda i, j: (i, 0))
        f_shape = jax.ShapeDtypeStruct((t, d), F32)
        out_specs = [q_spec, f_spec, f_spec, rest_spec]
        out_shape = [q_shape, f_shape, f_shape, rest_shape]
        args = (x, w_in_bf)
    return pl.pallas_call(
        functools.partial(_inproj_kernel, tm=tm, key_major=key_major),
        grid=(t // tm, N_IN_SPLITS),
        in_specs=in_specs,
        out_specs=out_specs,
        out_shape=out_shape,
        scratch_shapes=[pltpu.VMEM((tm, d), BF16)],
        compiler_params=_compiler_params(("arbitrary", "arbitrary")),
        name="inproj",
    )(*args)


def _sb_prompt_kernel(bias_ref, q_ref, k_ref, v_ref, o_ref, acc_ref, rem_ref, *, tq):
    hp = pl.program_id(1)
    qi = pl.program_id(2)
    lane = lax.broadcasted_iota(jnp.int32, (1, LANES), 1)
    row_head = lax.broadcasted_iota(jnp.int32, (LANES, KEY_BLOCK), 0) // HEAD_DIM
    nsm = _neg_suffix_sum_matrix()
    q2 = q_ref[...]
    heads = range(HEADS_PER_LANE_TILE)
    qm = [q2 * jnp.where(lane // HEAD_DIM == h, 1.0, 0.0).astype(BF16) for h in heads]
    v_mask = [jnp.where(row_head == h, 1.0, 0.0).astype(BF16) for h in heads]
    row_t = qi * tq + lax.broadcasted_iota(jnp.int32, (tq, 1), 0)

    acc_ref[...] = jnp.zeros_like(acc_ref)
    rem_ref[...] = jnp.zeros_like(rem_ref)

    def sweep(j, masked):
        k2t = k_ref[j]
        v2t = v_ref[j]
        if masked:
            valid = (j * KEY_BLOCK + lane) < row_t
        for h in heads:
            s = jnp.dot(qm[h], k2t, preferred_element_type=F32)
            z = s + bias_ref[HEADS_PER_LANE_TILE * hp + h]
            sp = _softplus(z)
            if masked:
                sp = jnp.where(valid, sp, 0.0)
            cm = jnp.dot(sp.astype(BF16), nsm, preferred_element_type=F32)
            a = jnp.exp(z + cm[:, :KEY_BLOCK] + rem_ref[h])
            if masked:
                a = jnp.where(valid, a, 0.0)
            rem_ref[h] = rem_ref[h] + cm[:, KEY_BLOCK:]
            acc_ref[...] += lax.dot_general(a.astype(BF16), v2t * v_mask[h], (((1,), (1,)), ((), ())),
                                            preferred_element_type=F32)

    blocks_per_q = tq // KEY_BLOCK
    for d in range(blocks_per_q):
        sweep(qi * blocks_per_q + (blocks_per_q - 1 - d), True)

    n_full = qi * blocks_per_q

    def body(jj, carry):
        sweep(n_full - 1 - jj, False)
        return carry

    lax.fori_loop(0, n_full, body, 0)
    o_ref[...] = acc_ref[...].astype(o_ref.dtype)


def _sb_prompt(q, kb, vb, sb_bias, tq):
    bsz, n_kb = kb.shape[:2]
    seq = n_kb * KEY_BLOCK
    n_pairs = N_HEADS // HEADS_PER_LANE_TILE
    rows_q = seq // tq
    kv_spec = pl.BlockSpec((None, n_kb, LANES, KEY_BLOCK), lambda b, p, i: (b, 0, p, 0))
    return pl.pallas_call(
        functools.partial(_sb_prompt_kernel, tq=tq),
        grid=(bsz, n_pairs, rows_q),
        in_specs=[
            pl.BlockSpec(memory_space=pltpu.SMEM),
            pl.BlockSpec((tq, LANES), lambda b, p, i: (b * rows_q + i, p)),
            kv_spec, kv_spec,
        ],
        out_specs=pl.BlockSpec((tq, LANES), lambda b, p, i: (b * rows_q + i, p)),
        out_shape=jax.ShapeDtypeStruct((bsz * seq, D_MODEL), BF16),
        scratch_shapes=[
            pltpu.VMEM((tq, LANES), F32),
            pltpu.VMEM((HEADS_PER_LANE_TILE, tq, LANES), F32),
        ],
        compiler_params=_compiler_params(("arbitrary", "arbitrary", "arbitrary")),
        name="sb_prompt",
    )(sb_bias, q, kb, vb)


def _sb_sample_kernel(pt_ref, bias_ref, q_ref, knew_ref, vnew_ref, *refs, pages_per_step):
    del pt_ref
    kv_refs = refs[: 2 * pages_per_step]
    o_ref, acc_ref, rem_ref = refs[2 * pages_per_step:]
    step = pl.program_id(1)
    head = lax.broadcasted_iota(jnp.int32, (N_HEADS, D_MODEL), 0)
    col_head = lax.broadcasted_iota(jnp.int32, (N_HEADS, D_MODEL), 1) // HEAD_DIM
    own = head == col_head
    q_rows = jnp.where(own, q_ref[...].astype(F32), 0.0).astype(BF16)
    bias = bias_ref[...]
    nsm = _neg_suffix_sum_matrix()

    @pl.when(step == 0)
    def _():
        n_new = knew_ref.shape[0]
        assert n_new == 1
        key_idx = lax.broadcasted_iota(jnp.int32, (N_HEADS, n_new), 1)
        valid = key_idx < (n_new - 1)
        s_new = jnp.sum(q_rows.astype(F32) * knew_ref[...], axis=-1, keepdims=True)
        z_new = s_new + bias[:, :1]
        sp_new = jnp.where(valid, _softplus(z_new), 0.0)
        a_new = jnp.where(valid, jnp.exp(z_new - sp_new), 0.0)
        acc_ref[...] = a_new * vnew_ref[...]
        rem_ref[...] = jnp.broadcast_to(-sp_new, rem_ref.shape)

    for g in range(pages_per_step):
        kp = kv_refs[2 * g][...].astype(BF16)
        vp = kv_refs[2 * g + 1][...].astype(BF16)
        s = jnp.dot(q_rows, kp, preferred_element_type=F32)
        z = s + bias
        sp = _softplus(z)
        cm = jnp.dot(sp.astype(BF16), nsm, preferred_element_type=F32)
        a = jnp.exp(z + cm[:, :KEY_BLOCK] + rem_ref[...])
        rem_ref[...] = rem_ref[...] + cm[:, KEY_BLOCK:]
        acc_ref[...] += lax.dot_general(a.astype(BF16), vp, (((1,), (1,)), ((), ())), preferred_element_type=F32)

    @pl.when(step == pl.num_programs(1) - 1)
    def _():
        o_ref[...] = jnp.sum(jnp.where(own, acc_ref[...], 0.0), axis=0, keepdims=True).astype(o_ref.dtype)


def _sb_sample(q3, k_new3, v_new3, bias_tile, k_pool, v_pool, page_table, pages_per_step):
    bsz = q3.shape[0]
    n_pages = page_table.shape[1]
    steps = n_pages // pages_per_step
    pt_flat = page_table.reshape(-1)

    def page_map(g):
        def index(b, s, pt):
            return (pt[b * n_pages + (n_pages - 1 - (s * pages_per_step + g))], 0, 0)
        return index

    kv_specs = []
    kv_args = []
    for g in range(pages_per_step):
        kv_specs += [pl.BlockSpec((None, D_MODEL, PAGE_SIZE), page_map(g))] * 2
        kv_args += [k_pool, v_pool]
    row_spec = pl.BlockSpec((None, 1, D_MODEL), lambda b, s, pt: (b, 0, 0))
    kernel = functools.partial(_sb_sample_kernel, pages_per_step=pages_per_step)
    grid_spec = pltpu.PrefetchScalarGridSpec(
        num_scalar_prefetch=1,
        grid=(bsz, steps),
        in_specs=[pl.BlockSpec((N_HEADS, LANES), lambda b, s, pt: (0, 0)), row_spec, row_spec, row_spec] + kv_specs,
        out_specs=row_spec,
        scratch_shapes=[pltpu.VMEM((N_HEADS, D_MODEL), F32), pltpu.VMEM((N_HEADS, KEY_BLOCK), F32)],
    )
    return pl.pallas_call(
        kernel,
        grid_spec=grid_spec,
        out_shape=jax.ShapeDtypeStruct((bsz, 1, D_MODEL), BF16),
        compiler_params=_compiler_params(("arbitrary", "arbitrary")),
        name="sb_sample",
    )(pt_flat, bias_tile, q3, k_new3, v_new3, *kv_args)


def _rglru_gates(xc, wra_ref, bra_ref, wri_ref, bri_ref, lam_ref):
    xb = xc.astype(BF16)
    r_parts = []
    i_parts = []
    for n in range(N_RNN_BLOCKS):
        xn = xb[:, n * RNN_BLOCK:(n + 1) * RNN_BLOCK]
        r_parts.append(jnp.dot(xn, wra_ref[n], preferred_element_type=F32))
        i_parts.append(jnp.dot(xn, wri_ref[n], preferred_element_type=F32))
    r = jax.nn.sigmoid(jnp.concatenate(r_parts, axis=-1) + bra_ref[...])
    i = jax.nn.sigmoid(jnp.concatenate(i_parts, axis=-1) + bri_ref[...])
    log_a = (-LRU_C * r) * _softplus(-lam_ref[...])
    a = jnp.exp(log_a)
    u = (xc * i) * jnp.sqrt(jnp.tanh(-log_a) * (a * a + 1.0))
    return a, u


def _rnn_scan_kernel(xr_ref, gr_ref, hist_ref, h0_ref, wc_ref, bc_ref, wra_ref, bra_ref, wri_ref, bri_ref,
                     lam_ref, out_ref, hlast_ref, xbuf_ref, a_ref, u_ref, h_ref, *, ts):
    st = pl.program_id(1)

    @pl.when(st == 0)
    def _():
        xbuf_ref[0:SUBLANES, :] = hist_ref[...]
        h_ref[...] = h0_ref[...]

    xbuf_ref[SUBLANES:SUBLANES + ts, :] = xr_ref[...]
    xc = bc_ref[...] + xbuf_ref[SUBLANES:SUBLANES + ts, :] * wc_ref[RNN_CONV_W - 1:RNN_CONV_W, :]
    for back in range(1, RNN_CONV_W):
        tap = RNN_CONV_W - 1 - back
        xc = xc + xbuf_ref[SUBLANES - back:SUBLANES - back + ts, :] * wc_ref[tap:tap + 1, :]
    xbuf_ref[0:SUBLANES, :] = xbuf_ref[ts:ts + SUBLANES, :]

    a, u = _rglru_gates(xc, wra_ref, bra_ref, wri_ref, bri_ref, lam_ref)
    a_ref[...] = a
    u_ref[...] = u

    def row(t, h):
        h = a_ref[pl.ds(t, 1), :] * h + u_ref[pl.ds(t, 1), :]
        u_ref[pl.ds(t, 1), :] = h
        return h

    h_last = lax.fori_loop(0, ts, row, h_ref[...], unroll=8)
    h_ref[...] = h_last
    hlast_ref[...] = h_last
    out_ref[...] = (jax.nn.gelu(gr_ref[...]) * u_ref[...]).astype(out_ref.dtype)


def _rnn_scan(rest, hist8, h0, wc, bc, wra, bra, wri, bri, lam, bsz, seq, ts):
    d = D_MODEL
    tiles = seq // ts
    vec = pl.BlockSpec((1, d), lambda b, s: (0, 0))
    blk = pl.BlockSpec((N_RNN_BLOCKS, RNN_BLOCK, RNN_BLOCK), lambda b, s: (0, 0, 0))
    kernel = functools.partial(_rnn_scan_kernel, ts=ts)
    return pl.pallas_call(
        kernel,
        grid=(bsz, tiles),
        in_specs=[
            pl.BlockSpec((ts, d), lambda b, s: (b * tiles + s, 0)),
            pl.BlockSpec((ts, d), lambda b, s: (b * tiles + s, 1)),
            pl.BlockSpec((None, SUBLANES, d), lambda b, s: (b, 0, 0)),
            pl.BlockSpec((None, 1, d), lambda b, s: (b, 0, 0)),
            pl.BlockSpec((RNN_CONV_W, d), lambda b, s: (0, 0)),
            vec, blk, vec, blk, vec, vec,
        ],
        out_specs=[
            pl.BlockSpec((ts, d), lambda b, s: (b * tiles + s, 0)),
            pl.BlockSpec((None, 1, d), lambda b, s: (b, 0, 0)),
        ],
        out_shape=[
            jax.ShapeDtypeStruct((bsz * seq, d), BF16),
            jax.ShapeDtypeStruct((bsz, 1, d), F32),
        ],
        scratch_shapes=[
            pltpu.VMEM((ts + SUBLANES, d), F32),
            pltpu.VMEM((ts, d), F32),
            pltpu.VMEM((ts, d), F32),
            pltpu.VMEM((1, d), F32),
        ],
        compiler_params=_compiler_params(("arbitrary", "arbitrary")),
        name="rnn_scan",
    )(rest, rest, hist8, h0, wc, bc, wra, bra, wri, bri, lam)


def _rnn_step_kernel(xr_ref, gr_ref, hist_ref, h0_ref, wc_ref, bc_ref, wra_ref, bra_ref, wri_ref, bri_ref,
                     lam_ref, out_ref, h_ref):
    xc = bc_ref[...] + xr_ref[...] * wc_ref[RNN_CONV_W - 1:RNN_CONV_W, :]
    for tap in range(RNN_CONV_W - 1):
        xc = xc + hist_ref[tap] * wc_ref[tap:tap + 1, :]
    a, u = _rglru_gates(xc, wra_ref, bra_ref, wri_ref, bri_ref, lam_ref)
    h = a * h0_ref[...] + u
    h_ref[...] = h
    out_ref[...] = (jax.nn.gelu(gr_ref[...]) * h).astype(out_ref.dtype)


def _rnn_step(rest, hist_t, h0, wc, bc, wra, bra, wri, bri, lam):
    bsz = rest.shape[0]
    d = D_MODEL
    vec = pl.BlockSpec((1, d), lambda i: (0, 0))
    blk = pl.BlockSpec((N_RNN_BLOCKS, RNN_BLOCK, RNN_BLOCK), lambda i: (0, 0, 0))
    return pl.pallas_call(
        _rnn_step_kernel,
        grid=(1,),
        in_specs=[
            pl.BlockSpec((bsz, d), lambda i: (0, 0)),
            pl.BlockSpec((bsz, d), lambda i: (0, 1)),
            pl.BlockSpec((RNN_CONV_W - 1, bsz, d), lambda i: (0, 0, 0)),
            pl.BlockSpec((bsz, d), lambda i: (0, 0)),
            pl.BlockSpec((RNN_CONV_W, d), lambda i: (0, 0)),
            vec, blk, vec, blk, vec, vec,
        ],
        out_specs=[pl.BlockSpec((bsz, d), lambda i: (0, 0)), pl.BlockSpec((bsz, d), lambda i: (0, 0))],
        out_shape=[jax.ShapeDtypeStruct((bsz, d), BF16), jax.ShapeDtypeStruct((bsz, d), F32)],
        compiler_params=_compiler_params(("arbitrary",)),
        name="rnn_step",
    )(rest, rest, hist_t, h0, wc, bc, wra, bra, wri, bri, lam)


def _merge_kernel(x_ref, att_ref, rnn_ref, ga_ref, gb_ref, wpa_ref, wpb_ref, wo_ref, g_ref, b_ref, o_ref):
    pa = jnp.dot(att_ref[...], wpa_ref[...], preferred_element_type=F32)
    pb = jnp.dot(rnn_ref[...], wpb_ref[...], preferred_element_type=F32)
    merged = jax.nn.sigmoid(ga_ref[...]) * pa + jax.nn.sigmoid(gb_ref[...]) * pb
    y = jnp.dot(merged.astype(BF16), wo_ref[...], preferred_element_type=F32)
    o_ref[...] = _layer_norm(DEEPNORM_ALPHA * x_ref[...] + y, g_ref[...], b_ref[...])


def _merge(x, att, rnn, rest, wpa, wpb, wo, g, b, tm):
    t = x.shape[0]
    d = D_MODEL
    row = pl.BlockSpec((tm, d), lambda i: (i, 0))
    mat = pl.BlockSpec((d, d), lambda i: (0, 0))
    vec = pl.BlockSpec((1, d), lambda i: (0, 0))
    return pl.pallas_call(
        _merge_kernel,
        grid=(t // tm,),
        in_specs=[row, row, row,
                  pl.BlockSpec((tm, d), lambda i: (i, 2)), pl.BlockSpec((tm, d), lambda i: (i, 3)),
                  mat, mat, mat, vec, vec],
        out_specs=row,
        out_shape=jax.ShapeDtypeStruct((t, d), F32),
        compiler_params=_compiler_params(("arbitrary",)),
        name="merge",
    )(x, att, rnn, rest, rest, wpa, wpb, wo, g, b)


def _ffn_kernel(*refs, tm, tf, rows_per_seq, per_row_history):
    if per_row_history:
        (x1_ref, wu_ref, wg_ref, wd_ref, wf_ref, bf_ref, g2_ref, b2_ref, pe_ref, wple_ref, wpleg_ref, bpleg_ref,
         g3_ref, b3_ref, hist_ref, o_ref, gate_ref, acc_ref, xb_ref) = refs
    else:
        (x1_ref, wu_ref, wg_ref, wd_ref, wf_ref, bf_ref, g2_ref, b2_ref, pe_ref, wple_ref, wpleg_ref, bpleg_ref,
         g3_ref, b3_ref, o_ref, gate_ref, acc_ref, xb_ref, gbuf_ref, gtail_ref) = refs
    i = pl.program_id(0)
    j = pl.program_id(1)

    @pl.when(j == 0)
    def _():
        xb_ref[...] = x1_ref[...].astype(BF16)
        acc_ref[...] = jnp.zeros_like(acc_ref)

    xb = xb_ref[...]
    up = jnp.dot(xb, wu_ref[...], preferred_element_type=F32)
    gate = jnp.dot(xb, wg_ref[...], preferred_element_type=F32)
    w_now = wf_ref[FFN_CONV_W - 1:FFN_CONV_W, :]
    if per_row_history:
        gate_ref[...] = gate
        gate_c = bf_ref[...] + gate * w_now
        for tap in range(FFN_CONV_W - 1):
            gate_c = gate_c + hist_ref[tap] * wf_ref[tap:tap + 1, :]
    else:
        first_of_seq = (i % (rows_per_seq // tm)) == 0

        @pl.when(first_of_seq)
        def _():
            gbuf_ref[0:SUBLANES, :] = jnp.zeros((SUBLANES, tf), F32)

        @pl.when(jnp.logical_not(first_of_seq))
        def _():
            gbuf_ref[0:SUBLANES, :] = gtail_ref[j]

        gbuf_ref[SUBLANES:SUBLANES + tm, :] = gate
        gate_c = bf_ref[...] + gate * w_now
        for back in range(1, FFN_CONV_W):
            tap = FFN_CONV_W - 1 - back
            gate_c = gate_c + gbuf_ref[SUBLANES - back:SUBLANES - back + tm, :] * wf_ref[tap:tap + 1, :]
        tail = gate[tm - SUBLANES:tm, :]
        gtail_ref[j] = tail
        gate_ref[...] = tail
    hidden = (jax.nn.gelu(gate_c) * up).astype(BF16)
    acc_ref[...] += jnp.dot(hidden, wd_ref[...], preferred_element_type=F32)

    @pl.when(j == pl.num_programs(1) - 1)
    def _():
        x2 = _layer_norm(DEEPNORM_ALPHA * x1_ref[...] + acc_ref[...], g2_ref[...], b2_ref[...])
        emb = jnp.dot(pe_ref[...].astype(BF16), wple_ref[...], preferred_element_type=F32)
        gate_e = jax.nn.sigmoid(jnp.dot(x2.astype(BF16), wpleg_ref[...], preferred_element_type=F32) + bpleg_ref[...])
        o_ref[...] = _layer_norm(DEEPNORM_ALPHA * x2 + emb * gate_e, g3_ref[...], b3_ref[...])


def _ffn(x1, pe, wup, wdown, wf, bfc, g2, b2, wple, wpleg, bpleg, g3, b3, tm, tf, rows_per_seq, hist_t=None):
    t = x1.shape[0]
    d = D_MODEL
    n_chunks = D_FF // tf
    per_row_history = hist_t is not None
    row = pl.BlockSpec((tm, d), lambda i, j: (i, 0))
    vec = pl.BlockSpec((1, d), lambda i, j: (0, 0))
    fvec = pl.BlockSpec((1, tf), lambda i, j: (0, j))
    in_specs = [
        row,
        pl.BlockSpec((d, tf), lambda i, j: (0, j)),
        pl.BlockSpec((d, tf), lambda i, j: (0, n_chunks + j)),
        pl.BlockSpec((tf, d), lambda i, j: (j, 0)),
        pl.BlockSpec((FFN_CONV_W, tf), lambda i, j: (0, j)),
        fvec, vec, vec,
        pl.BlockSpec((tm, D_PLE), lambda i, j: (i, 0)),
        pl.BlockSpec((D_PLE, d), lambda i, j: (0, 0)),
        pl.BlockSpec((d, d), lambda i, j: (0, 0)),
        vec, vec, vec,
    ]
    args = [x1, wup, wup, wdown, wf, bfc, g2, b2, pe, wple, wpleg, bpleg, g3, b3]
    scratch = [pltpu.VMEM((tm, d), F32), pltpu.VMEM((tm, d), BF16)]
    if per_row_history:
        in_specs.append(pl.BlockSpec((FFN_CONV_W - 1, tm, tf), lambda i, j: (0, i, j)))
        args.append(hist_t)
        gate_spec = pl.BlockSpec((tm, tf), lambda i, j: (i, j))
        gate_shape = jax.ShapeDtypeStruct((t, D_FF), F32)
    else:
        gate_spec = pl.BlockSpec((None, SUBLANES, tf), lambda i, j: (i, 0, j))
        gate_shape = jax.ShapeDtypeStruct((t // tm, SUBLANES, D_FF), F32)
        scratch += [pltpu.VMEM((tm + SUBLANES, tf), F32), pltpu.VMEM((n_chunks, SUBLANES, tf), F32)]
    kernel = functools.partial(_ffn_kernel, tm=tm, tf=tf, rows_per_seq=rows_per_seq, per_row_history=per_row_history)
    return pl.pallas_call(
        kernel,
        grid=(t // tm, n_chunks),
        in_specs=in_specs,
        out_specs=[row, gate_spec],
        out_shape=[jax.ShapeDtypeStruct((t, d), F32), gate_shape],
        scratch_shapes=scratch,
        compiler_params=_compiler_params(("arbitrary", "arbitrary")),
        name="ffn",
    )(*args)


def kernel(x_prompt, x_sample, p_prompt, p_sample, cache_k, cache_v, page_table, state_rnn_conv, state_rnn_h, state_ffn_conv, w_in, sb_bias, w_rconv, b_rconv, w_ra, b_ra, w_ri, b_ri, lru_lambda, w_pa, w_pb, w_o, ln1_g, ln1_b, w_up, w_fconv, b_fconv, w_down, ln2_g, ln2_b, w_ple, w_pleg, b_pleg, ln3_g, ln3_b):
    assert w_in.shape[0] == DEPTH == 1
    bsz, seq, d = x_prompt.shape
    dec = x_sample.shape[0]
    assert x_sample.shape[1] == 1 and d == D_MODEL
    n_phys = cache_k.shape[1]

    row_vec = lambda a: a.reshape(1, -1)
    w_in_bf = w_in[0].astype(BF16)
    wra, wri = w_ra[0].astype(BF16), w_ri[0].astype(BF16)
    wpa, wpb, wo = w_pa[0].astype(BF16), w_pb[0].astype(BF16), w_o[0].astype(BF16)
    wup, wdown = w_up[0].astype(BF16), w_down[0].astype(BF16)
    wple, wpleg = w_ple[0].astype(BF16), w_pleg[0].astype(BF16)
    rnn_w = (w_rconv[0], row_vec(b_rconv[0]), wra, row_vec(b_ra[0]), wri, row_vec(b_ri[0]), row_vec(lru_lambda[0]))
    ffn_w = (wup, wdown, w_fconv[0], row_vec(b_fconv[0]), row_vec(ln2_g[0]), row_vec(ln2_b[0]),
             wple, wpleg, row_vec(b_pleg[0]), row_vec(ln3_g[0]), row_vec(ln3_b[0]))
    merge_w = (wpa, wpb, wo, row_vec(ln1_g[0]), row_vec(ln1_b[0]))

    xp = x_prompt.reshape(bsz * seq, d)
    wt_kv_bf = jnp.transpose(w_in[0][:, d:3 * d]).astype(BF16)
    q_p, kt_p, vt_p, kb_p, vb_p, rest_p = _inproj(xp, w_in_bf, tm=512, wt_kv_bf=wt_kv_bf, rows_per_seq=seq)
    att_p = _sb_prompt(q_p, kb_p, vb_p, sb_bias[0], tq=128)
    rnn_p, h_p = _rnn_scan(rest_p, jnp.zeros((bsz, SUBLANES, d), F32), jnp.zeros((bsz, 1, d), F32), *rnn_w,
                           bsz=bsz, seq=seq, ts=256)
    x1_p = _merge(xp, att_p, rnn_p, rest_p, *merge_w, tm=512)
    ffn_tm = 512
    y_p, gate_tails = _ffn(x1_p, p_prompt[0].reshape(bsz * seq, D_PLE), *ffn_w, tm=ffn_tm, tf=512, rows_per_seq=seq)
    rest_p4 = rest_p.reshape(bsz, seq, 4 * d)
    rconv_p = rest_p4[:, seq - (RNN_CONV_W - 1):, :d]
    fconv_p = gate_tails.reshape(bsz, seq // ffn_tm, SUBLANES, D_FF)[:, -1, SUBLANES - (FFN_CONV_W - 1):, :]
    hd = (N_HEADS, HEAD_DIM)
    k_p = jnp.transpose(kt_p.reshape(bsz, *hd, seq), (0, 3, 1, 2))
    v_p = jnp.transpose(vt_p.reshape(bsz, *hd, seq), (0, 3, 1, 2))

    xs = x_sample.reshape(dec, d)
    q_s, k_s, v_s, rest_s = _inproj(xs, w_in_bf, tm=dec)
    bias_tile = jnp.broadcast_to(sb_bias[0][:, None], (N_HEADS, LANES))
    k_pool = jnp.transpose(cache_k[0], (0, 2, 3, 1)).reshape(n_phys, d, PAGE_SIZE)
    v_pool = jnp.transpose(cache_v[0], (0, 2, 3, 1)).reshape(n_phys, d, PAGE_SIZE)
    att_s = _sb_sample(q_s.reshape(dec, 1, d), k_s.reshape(dec, 1, d), v_s.reshape(dec, 1, d), bias_tile,
                       k_pool, v_pool, page_table, pages_per_step=4).reshape(dec, d)
    rnn_s, h_s = _rnn_step(rest_s, jnp.swapaxes(state_rnn_conv[0], 0, 1), state_rnn_h[0], *rnn_w)
    x1_s = _merge(xs, att_s, rnn_s, rest_s, *merge_w, tm=dec)
    y_s, gate_s = _ffn(x1_s, p_sample[0].reshape(dec, D_PLE), *ffn_w, tm=dec, tf=512, rows_per_seq=1,
                       hist_t=jnp.swapaxes(state_ffn_conv[0], 0, 1))
    rconv_s = jnp.concatenate([state_rnn_conv[0][:, 1:], rest_s[:, None, :d]], axis=1)
    fconv_s = jnp.concatenate([state_ffn_conv[0][:, 1:], gate_s[:, None, :]], axis=1)

    return (y_p.reshape(bsz, seq, d), y_s.reshape(dec, 1, d),
            k_p[None], v_p[None],
            rconv_p[None], h_p.reshape(1, bsz, d), fconv_p[None],
            k_s.reshape(1, dec, 1, *hd), v_s.reshape(1, dec, 1, *hd),
            rconv_s[None], h_s[None], fconv_s[None])
```

```python
import functools

import jax
import jax.numpy as jnp
from jax import lax
from jax.experimental import pallas as pl
from jax.experimental.pallas import tpu as pltpu

F32 = jnp.float32
BF16 = jnp.bfloat16

D_MODEL = 1024
N_HEADS = 16
HEAD_DIM = 64
N_RNN_BLOCKS = 8
RNN_BLOCK = D_MODEL // N_RNN_BLOCKS
RNN_CONV_W = 4
LRU_C = 8.0
D_FF = 3 * D_MODEL
FFN_CONV_W = 3
D_PLE = 256
PAGE_SIZE = 128
LN_EPS = 1e-5
DEPTH = 1
DEEPNORM_ALPHA = (2.0 * DEPTH) ** 0.25
N_IN_SPLITS = 7

LANES = 128
SUBLANES = 8
HEADS_PER_LANE_TILE = LANES // HEAD_DIM
KEY_BLOCK = 256
LOG2E = 1.4426950408889634
VMEM_LIMIT_BYTES = 56 * 1024 * 1024


def _compiler_params(semantics):
    return pltpu.CompilerParams(dimension_semantics=semantics, vmem_limit_bytes=VMEM_LIMIT_BYTES)


def _layer_norm(x, g, b):
    mu = jnp.mean(x, axis=-1, keepdims=True)
    xc = x - mu
    var = jnp.mean(xc * xc, axis=-1, keepdims=True)
    return xc * lax.rsqrt(var + LN_EPS) * g + b


def _softplus(z):
    return jnp.maximum(z, 0.0) + jnp.log(1.0 + jnp.exp(-jnp.abs(z)))


def _softplus2(z2):
    return jnp.maximum(z2, 0.0) + jnp.log2(1.0 + jnp.exp2(-jnp.abs(z2)))


def _neg_suffix_sum_matrix(n):
    j = lax.broadcasted_iota(jnp.int32, (n, n), 0)
    s = lax.broadcasted_iota(jnp.int32, (n, n), 1)
    return jnp.where(j >= s, -1.0, 0.0).astype(BF16)


def _inproj_kernel(*refs, tm, key_major):
    if key_major:
        x_ref, w_ref, wt_ref, q_ref, kf_ref, vf_ref, kb_ref, vb_ref, rest_ref, xb_ref = refs
    else:
        x_ref, w_ref, q_ref, kf_ref, vf_ref, rest_ref, xb_ref = refs
    j = pl.program_id(1)

    @pl.when(j == 0)
    def _():
        xb_ref[...] = x_ref[...].astype(BF16)

    def project():
        return jnp.dot(xb_ref[...], w_ref[...], preferred_element_type=F32)

    def project_key_major():
        return lax.dot_general(wt_ref[...], xb_ref[...], (((1,), (1,)), ((), ())), preferred_element_type=F32)

    @pl.when(j == 0)
    def _():
        q_ref[...] = (project() * (LOG2E * HEAD_DIM ** -0.5)).astype(BF16)

    @pl.when(j == 1)
    def _():
        if key_major:
            kt = project_key_major()
            kf_ref[...] = kt
            for kb in range(tm // KEY_BLOCK):
                kb_ref[kb] = kt[:, kb * KEY_BLOCK:(kb + 1) * KEY_BLOCK].astype(BF16)
        else:
            kf_ref[...] = project()

    @pl.when(j == 2)
    def _():
        if key_major:
            vf_ref[...] = project_key_major()
            vb_ref[...] = project().astype(BF16)
        else:
            vf_ref[...] = project()

    @pl.when(j >= 3)
    def _():
        rest_ref[...] = project()


def _inproj(x, w_in_bf, tm, wt_kv_bf=None, rows_per_seq=None):
    t = x.shape[0]
    d = D_MODEL
    key_major = wt_kv_bf is not None
    x_spec = pl.BlockSpec((tm, d), lambda i, j: (i, 0))
    q_spec = pl.BlockSpec((tm, d), lambda i, j: (i, 0))
    rest_spec = pl.BlockSpec((tm, d), lambda i, j: (i, jnp.maximum(j - 3, 0)))
    q_shape = jax.ShapeDtypeStruct((t, d), BF16)
    rest_shape = jax.ShapeDtypeStruct((t, 4 * d), F32)
    if key_major:
        tiles = rows_per_seq // tm
        bsz = t // rows_per_seq
        kb_per_tile = tm // KEY_BLOCK
        in_specs = [
            x_spec,
            pl.BlockSpec((d, d), lambda i, j: (0, jnp.where(j == 1, 0, j))),
            pl.BlockSpec((d, d), lambda i, j: (jnp.clip(j - 1, 0, 1), 0)),
        ]
        f_spec = pl.BlockSpec((None, d, tm), lambda i, j: (i // tiles, 0, i % tiles))
        kb_spec = pl.BlockSpec((None, kb_per_tile, d, KEY_BLOCK), lambda i, j: (i // tiles, i % tiles, 0, 0))
        f_shape = jax.ShapeDtypeStruct((bsz, d, rows_per_seq), F32)
        kb_shape = jax.ShapeDtypeStruct((bsz, rows_per_seq // KEY_BLOCK, d, KEY_BLOCK), BF16)
        out_specs = [q_spec, f_spec, f_spec, kb_spec, q_spec, rest_spec]
        out_shape = [q_shape, f_shape, f_shape, kb_shape, q_shape, rest_shape]
        args = (x, w_in_bf, wt_kv_bf)
    else:
        in_specs = [x_spec, pl.BlockSpec((d, d), lambda i, j: (0, j))]
        f_spec = pl.BlockSpec((tm, d), lambda i, j: (i, 0))
        f_shape = jax.ShapeDtypeStruct((t, d), F32)
        out_specs = [q_spec, f_spec, f_spec, rest_spec]
        out_shape = [q_shape, f_shape, f_shape, rest_shape]
        args = (x, w_in_bf)
    return pl.pallas_call(
        functools.partial(_inproj_kernel, tm=tm, key_major=key_major),
        grid=(t // tm, N_IN_SPLITS),
        in_specs=in_specs,
        out_specs=out_specs,
        out_shape=out_shape,
        scratch_shapes=[pltpu.VMEM((tm, d), BF16)],
        compiler_params=_compiler_params(("arbitrary", "arbitrary")),
        name="inproj",
    )(*args)


def _sb_prompt_kernel(bias_ref, q_ref, k_ref, v_ref, o_ref, acc_ref, rem_ref, *, tq, pairs_per_step):
    group = pl.program_id(1)
    qi = pl.program_id(2)
    heads = range(HEADS_PER_LANE_TILE)
    pairs = range(pairs_per_step)
    kb = KEY_BLOCK
    col_head = lax.broadcasted_iota(jnp.int32, (1, HEADS_PER_LANE_TILE * kb), 1) // kb
    first_head = [HEADS_PER_LANE_TILE * (group * pairs_per_step + p) for p in pairs]
    bias_row = [jnp.where(col_head == 0, bias_ref[first_head[p]], bias_ref[first_head[p] + 1]) for p in pairs]
    k_row_head = lax.broadcasted_iota(jnp.int32, (LANES, kb), 0) // HEAD_DIM
    v_lane_head = lax.broadcasted_iota(jnp.int32, (kb, LANES), 1) // HEAD_DIM
    k_mask = [jnp.where(k_row_head == h, 1.0, 0.0).astype(BF16) for h in heads]
    v_mask = [jnp.where(v_lane_head == h, 1.0, 0.0).astype(BF16) for h in heads]
    nsm = _neg_suffix_sum_matrix(kb)
    q_all = q_ref[...]
    key_in_block = lax.broadcasted_iota(jnp.int32, (1, kb), 1)
    row_t = qi * tq + lax.broadcasted_iota(jnp.int32, (tq, 1), 0)

    acc_ref[...] = jnp.zeros_like(acc_ref)
    rem_ref[...] = jnp.zeros_like(rem_ref)

    def sweep(j, masked):
        k_all = k_ref[j]
        v_all = v_ref[pl.ds(pl.multiple_of(j * kb, kb), kb), :]
        if masked:
            valid = (j * kb + key_in_block) < row_t
            valid_pair = jnp.concatenate([valid] * HEADS_PER_LANE_TILE, axis=1)
        zs = []
        for p in pairs:
            k2t = k_all[p * LANES:(p + 1) * LANES]
            w = jnp.concatenate([k2t * k_mask[h] for h in heads], axis=1)
            zs.append(jnp.dot(q_all[:, p * LANES:(p + 1) * LANES], w, preferred_element_type=F32) + bias_row[p])
        sps = [_softplus2(z) for z in zs]
        if masked:
            sps = [jnp.where(valid_pair, sp, 0.0) for sp in sps]
        css = []
        for sp in sps:
            sp_rows = jnp.concatenate([sp[:, h * kb:(h + 1) * kb] for h in heads], axis=0).astype(BF16)
            css.append(jnp.dot(sp_rows, nsm, preferred_element_type=F32))
        for p in pairs:
            a_parts = []
            for h in heads:
                rem = rem_ref[HEADS_PER_LANE_TILE * p + h]
                arg = (zs[p][:, h * kb:(h + 1) * kb] + css[p][h * tq:(h + 1) * tq]
                       + jnp.tile(rem, (1, kb // LANES)))
                a_h = jnp.exp2(arg)
                if masked:
                    a_h = jnp.where(valid, a_h, 0.0)
                a_parts.append(a_h.astype(BF16))
                rem_ref[HEADS_PER_LANE_TILE * p + h] = rem - jnp.sum(
                    sps[p][:, h * kb:(h + 1) * kb], axis=-1, keepdims=True)
            a = jnp.concatenate(a_parts, axis=1)
            v2 = v_all[:, p * LANES:(p + 1) * LANES]
            v_bd = jnp.concatenate([v2 * v_mask[h] for h in heads], axis=0)
            acc_ref[:, p * LANES:(p + 1) * LANES] += jnp.dot(a, v_bd, preferred_element_type=F32)

    blocks_per_q = tq // kb
    for d in range(blocks_per_q):
        sweep(qi * blocks_per_q + (blocks_per_q - 1 - d), True)

    n_full = qi * blocks_per_q

    def body(jj, carry):
        sweep(n_full - 1 - jj, False)
        return carry

    lax.fori_loop(0, n_full, body, 0)
    o_ref[...] = acc_ref[...].astype(o_ref.dtype)


def _sb_prompt(q, kb, vb, bias2, tq, pairs_per_step):
    bsz, n_kb = kb.shape[:2]
    seq = n_kb * KEY_BLOCK
    width = pairs_per_step * LANES
    groups = D_MODEL // width
    rows_q = seq // tq
    return pl.pallas_call(
        functools.partial(_sb_prompt_kernel, tq=tq, pairs_per_step=pairs_per_step),
        grid=(bsz, groups, rows_q),
        in_specs=[
            pl.BlockSpec(memory_space=pltpu.SMEM),
            pl.BlockSpec((tq, width), lambda b, g, i: (b * rows_q + i, g)),
            pl.BlockSpec((None, n_kb, width, KEY_BLOCK), lambda b, g, i: (b, 0, g, 0)),
            pl.BlockSpec((seq, width), lambda b, g, i: (b, g)),
        ],
        out_specs=pl.BlockSpec((tq, width), lambda b, g, i: (b * rows_q + i, g)),
        out_shape=jax.ShapeDtypeStruct((bsz * seq, D_MODEL), BF16),
        scratch_shapes=[
            pltpu.VMEM((tq, width), F32),
            pltpu.VMEM((pairs_per_step * HEADS_PER_LANE_TILE, tq, LANES), F32),
        ],
        compiler_params=_compiler_params(("arbitrary", "arbitrary", "arbitrary")),
        name="sb_prompt",
    )(bias2, q, kb, vb)


def _sb_sample_kernel(pt_ref, bias_ref, q_ref, knew_ref, vnew_ref, *refs, pages_per_step):
    del pt_ref
    kv_refs = refs[: 2 * pages_per_step]
    o_ref, acc_ref, rem_ref = refs[2 * pages_per_step:]
    step = pl.program_id(1)
    head = lax.broadcasted_iota(jnp.int32, (N_HEADS, D_MODEL), 0)
    col_head = lax.broadcasted_iota(jnp.int32, (N_HEADS, D_MODEL), 1) // HEAD_DIM
    own = head == col_head
    q_rows = jnp.where(own, q_ref[...].astype(F32), 0.0).astype(BF16)
    bias = bias_ref[...]
    nsm = _neg_suffix_sum_matrix(PAGE_SIZE)

    @pl.when(step == 0)
    def _():
        n_new = knew_ref.shape[0]
        assert n_new == 1
        key_idx = lax.broadcasted_iota(jnp.int32, (N_HEADS, n_new), 1)
        valid = key_idx < (n_new - 1)
        s_new = jnp.sum(q_rows.astype(F32) * knew_ref[...], axis=-1, keepdims=True)
        z_new = s_new + bias[:, :1]
        sp_new = jnp.where(valid, _softplus2(z_new), 0.0)
        a_new = jnp.where(valid, jnp.exp2(z_new - sp_new), 0.0)
        acc_ref[...] = a_new * vnew_ref[...]
        rem_ref[...] = jnp.broadcast_to(-sp_new, rem_ref.shape)

    pages = range(pages_per_step)
    zs = [jnp.dot(q_rows, kv_refs[2 * g][...].astype(BF16), preferred_element_type=F32) + bias for g in pages]
    sps = [_softplus2(z) for z in zs]
    css = [jnp.dot(sp.astype(BF16), nsm, preferred_element_type=F32) for sp in sps]
    tots = [jnp.sum(sp, axis=-1, keepdims=True) for sp in sps]
    rem = rem_ref[...]
    acc = acc_ref[...]
    for g in pages:
        a = jnp.exp2(zs[g] + css[g] + rem)
        rem = rem - tots[g]
        acc = acc + lax.dot_general(a.astype(BF16), kv_refs[2 * g + 1][...].astype(BF16), (((1,), (1,)), ((), ())),
                                    preferred_element_type=F32)
    rem_ref[...] = rem
    acc_ref[...] = acc

    @pl.when(step == pl.num_programs(1) - 1)
    def _():
        o_ref[...] = jnp.sum(jnp.where(own, acc_ref[...], 0.0), axis=0, keepdims=True).astype(o_ref.dtype)


def _sb_sample(q3, k_new3, v_new3, bias_tile, k_pool, v_pool, page_table, pages_per_step):
    bsz = q3.shape[0]
    n_pages = page_table.shape[1]
    steps = n_pages // pages_per_step
    pt_flat = page_table.reshape(-1)

    def page_map(g):
        def index(b, s, pt):
            return (pt[b * n_pages + (n_pages - 1 - (s * pages_per_step + g))], 0, 0)
        return index

    kv_specs = []
    kv_args = []
    for g in range(pages_per_step):
        kv_specs += [pl.BlockSpec((None, D_MODEL, PAGE_SIZE), page_map(g))] * 2
        kv_args += [k_pool, v_pool]
    row_spec = pl.BlockSpec((None, 1, D_MODEL), lambda b, s, pt: (b, 0, 0))
    kernel = functools.partial(_sb_sample_kernel, pages_per_step=pages_per_step)
    grid_spec = pltpu.PrefetchScalarGridSpec(
        num_scalar_prefetch=1,
        grid=(bsz, steps),
        in_specs=[pl.BlockSpec((N_HEADS, LANES), lambda b, s, pt: (0, 0)), row_spec, row_spec, row_spec] + kv_specs,
        out_specs=row_spec,
        scratch_shapes=[pltpu.VMEM((N_HEADS, D_MODEL), F32), pltpu.VMEM((N_HEADS, PAGE_SIZE), F32)],
    )
    return pl.pallas_call(
        kernel,
        grid_spec=grid_spec,
        out_shape=jax.ShapeDtypeStruct((bsz, 1, D_MODEL), BF16),
        compiler_params=_compiler_params(("arbitrary", "arbitrary")),
        name="sb_sample",
    )(pt_flat, bias_tile, q3, k_new3, v_new3, *kv_args)


def _rglru_gates(xc, wra_ref, bra_ref, wri_ref, bri_ref, lam_ref):
    xb = xc.astype(BF16)
    r_parts = []
    i_parts = []
    for n in range(N_RNN_BLOCKS):
        xn = xb[:, n * RNN_BLOCK:(n + 1) * RNN_BLOCK]
        r_parts.append(jnp.dot(xn, wra_ref[n], preferred_element_type=F32))
        i_parts.append(jnp.dot(xn, wri_ref[n], preferred_element_type=F32))
    r = jax.nn.sigmoid(jnp.concatenate(r_parts, axis=-1) + bra_ref[...])
    i = jax.nn.sigmoid(jnp.concatenate(i_parts, axis=-1) + bri_ref[...])
    log_a = (-LRU_C * r) * _softplus(-lam_ref[...])
    a = jnp.exp(log_a)
    u = (xc * i) * jnp.sqrt(jnp.tanh(-log_a) * (a * a + 1.0))
    return a, u


def _rnn_scan_kernel(xr_ref, gr_ref, hist_ref, h0_ref, wc_ref, bc_ref, wra_ref, bra_ref, wri_ref, bri_ref,
                     lam_ref, out_ref, hlast_ref, xbuf_ref, a_ref, u_ref, h_ref, *, ts):
    st = pl.program_id(1)

    @pl.when(st == 0)
    def _():
        xbuf_ref[0:SUBLANES, :] = hist_ref[...]
        h_ref[...] = h0_ref[...]

    xbuf_ref[SUBLANES:SUBLANES + ts, :] = xr_ref[...]
    xc = bc_ref[...] + xbuf_ref[SUBLANES:SUBLANES + ts, :] * wc_ref[RNN_CONV_W - 1:RNN_CONV_W, :]
    for back in range(1, RNN_CONV_W):
        tap = RNN_CONV_W - 1 - back
        xc = xc + xbuf_ref[SUBLANES - back:SUBLANES - back + ts, :] * wc_ref[tap:tap + 1, :]
    xbuf_ref[0:SUBLANES, :] = xbuf_ref[ts:ts + SUBLANES, :]

    a, u = _rglru_gates(xc, wra_ref, bra_ref, wri_ref, bri_ref, lam_ref)
    a_ref[...] = a
    u_ref[...] = u

    def row(t, h):
        h = a_ref[pl.ds(t, 1), :] * h + u_ref[pl.ds(t, 1), :]
        u_ref[pl.ds(t, 1), :] = h
        return h

    h_last = lax.fori_loop(0, ts, row, h_ref[...], unroll=8)
    h_ref[...] = h_last
    hlast_ref[...] = h_last
    out_ref[...] = (jax.nn.gelu(gr_ref[...]) * u_ref[...]).astype(out_ref.dtype)


def _rnn_scan(rest, hist8, h0, wc, bc, wra, bra, wri, bri, lam, bsz, seq, ts):
    d = D_MODEL
    tiles = seq // ts
    vec = pl.BlockSpec((1, d), lambda b, s: (0, 0))
    blk = pl.BlockSpec((N_RNN_BLOCKS, RNN_BLOCK, RNN_BLOCK), lambda b, s: (0, 0, 0))
    kernel = functools.partial(_rnn_scan_kernel, ts=ts)
    return pl.pallas_call(
        kernel,
        grid=(bsz, tiles),
        in_specs=[
            pl.BlockSpec((ts, d), lambda b, s: (b * tiles + s, 0)),
            pl.BlockSpec((ts, d), lambda b, s: (b * tiles + s, 1)),
            pl.BlockSpec((None, SUBLANES, d), lambda b, s: (b, 0, 0)),
            pl.BlockSpec((None, 1, d), lambda b, s: (b, 0, 0)),
            pl.BlockSpec((RNN_CONV_W, d), lambda b, s: (0, 0)),
            vec, blk, vec, blk, vec, vec,
        ],
        out_specs=[
            pl.BlockSpec((ts, d), lambda b, s: (b * tiles + s, 0)),
            pl.BlockSpec((None, 1, d), lambda b, s: (b, 0, 0)),
        ],
        out_shape=[
            jax.ShapeDtypeStruct((bsz * seq, d), BF16),
            jax.ShapeDtypeStruct((bsz, 1, d), F32),
        ],
        scratch_shapes=[
            pltpu.VMEM((ts + SUBLANES, d), F32),
            pltpu.VMEM((ts, d), F32),
            pltpu.VMEM((ts, d), F32),
            pltpu.VMEM((1, d), F32),
        ],
        compiler_params=_compiler_params(("arbitrary", "arbitrary")),
        name="rnn_scan",
    )(rest, rest, hist8, h0, wc, bc, wra, bra, wri, bri, lam)


def _rnn_step_kernel(xr_ref, gr_ref, hist_ref, h0_ref, wc_ref, bc_ref, wra_ref, bra_ref, wri_ref, bri_ref,
                     lam_ref, out_ref, h_ref):
    xc = bc_ref[...] + xr_ref[...] * wc_ref[RNN_CONV_W - 1:RNN_CONV_W, :]
    for tap in range(RNN_CONV_W - 1):
        xc = xc + hist_ref[tap] * wc_ref[tap:tap + 1, :]
    a, u = _rglru_gates(xc, wra_ref, bra_ref, wri_ref, bri_ref, lam_ref)
    h = a * h0_ref[...] + u
    h_ref[...] = h
    out_ref[...] = (jax.nn.gelu(gr_ref[...]) * h).astype(out_ref.dtype)


def _rnn_step(rest, hist_t, h0, wc, bc, wra, bra, wri, bri, lam):
    bsz = rest.shape[0]
    d = D_MODEL
    vec = pl.BlockSpec((1, d), lambda i: (0, 0))
    blk = pl.BlockSpec((N_RNN_BLOCKS, RNN_BLOCK, RNN_BLOCK), lambda i: (0, 0, 0))
    return pl.pallas_call(
        _rnn_step_kernel,
        grid=(1,),
        in_specs=[
            pl.BlockSpec((bsz, d), lambda i: (0, 0)),
            pl.BlockSpec((bsz, d), lambda i: (0, 1)),
            pl.BlockSpec((RNN_CONV_W - 1, bsz, d), lambda i: (0, 0, 0)),
            pl.BlockSpec((bsz, d), lambda i: (0, 0)),
            pl.BlockSpec((RNN_CONV_W, d), lambda i: (0, 0)),
            vec, blk, vec, blk, vec, vec,
        ],
        out_specs=[pl.BlockSpec((bsz, d), lambda i: (0, 0)), pl.BlockSpec((bsz, d), lambda i: (0, 0))],
        out_shape=[jax.ShapeDtypeStruct((bsz, d), BF16), jax.ShapeDtypeStruct((bsz, d), F32)],
        compiler_params=_compiler_params(("arbitrary",)),
        name="rnn_step",
    )(rest, rest, hist_t, h0, wc, bc, wra, bra, wri, bri, lam)


def _merge_kernel(x_ref, att_ref, rnn_ref, ga_ref, gb_ref, wpa_ref, wpb_ref, wo_ref, g_ref, b_ref, o_ref):
    pa = jnp.dot(att_ref[...], wpa_ref[...], preferred_element_type=F32)
    pb = jnp.dot(rnn_ref[...], wpb_ref[...], preferred_element_type=F32)
    merged = jax.nn.sigmoid(ga_ref[...]) * pa + jax.nn.sigmoid(gb_ref[...]) * pb
    y = jnp.dot(merged.astype(BF16), wo_ref[...], preferred_element_type=F32)
    o_ref[...] = _layer_norm(DEEPNORM_ALPHA * x_ref[...] + y, g_ref[...], b_ref[...])


def _merge(x, att, rnn, rest, wpa, wpb, wo, g, b, tm):
    t = x.shape[0]
    d = D_MODEL
    row = pl.BlockSpec((tm, d), lambda i: (i, 0))
    mat = pl.BlockSpec((d, d), lambda i: (0, 0))
    vec = pl.BlockSpec((1, d), lambda i: (0, 0))
    return pl.pallas_call(
        _merge_kernel,
        grid=(t // tm,),
        in_specs=[row, row, row,
                  pl.BlockSpec((tm, d), lambda i: (i, 2)), pl.BlockSpec((tm, d), lambda i: (i, 3)),
                  mat, mat, mat, vec, vec],
        out_specs=row,
        out_shape=jax.ShapeDtypeStruct((t, d), F32),
        compiler_params=_compiler_params(("arbitrary",)),
        name="merge",
    )(x, att, rnn, rest, rest, wpa, wpb, wo, g, b)


def _ffn_kernel(*refs, tm, tf, rows_per_seq, per_row_history):
    if per_row_history:
        (x1_ref, wu_ref, wg_ref, wd_ref, wf_ref, bf_ref, g2_ref, b2_ref, pe_ref, wple_ref, wpleg_ref, bpleg_ref,
         g3_ref, b3_ref, hist_ref, o_ref, gate_ref, acc_ref, xb_ref) = refs
    else:
        (x1_ref, wu_ref, wg_ref, wd_ref, wf_ref, bf_ref, g2_ref, b2_ref, pe_ref, wple_ref, wpleg_ref, bpleg_ref,
         g3_ref, b3_ref, o_ref, gate_ref, acc_ref, xb_ref, gbuf_ref, gtail_ref) = refs
    i = pl.program_id(0)
    j = pl.program_id(1)

    @pl.when(j == 0)
    def _():
        xb_ref[...] = x1_ref[...].astype(BF16)
        acc_ref[...] = jnp.zeros_like(acc_ref)

    xb = xb_ref[...]
    up = jnp.dot(xb, wu_ref[...], preferred_element_type=F32)
    gate = jnp.dot(xb, wg_ref[...], preferred_element_type=F32)
    w_now = wf_ref[FFN_CONV_W - 1:FFN_CONV_W, :]
    if per_row_history:
        gate_ref[...] = gate
        gate_c = bf_ref[...] + gate * w_now
        for tap in range(FFN_CONV_W - 1):
            gate_c = gate_c + hist_ref[tap] * wf_ref[tap:tap + 1, :]
    else:
        first_of_seq = (i % (rows_per_seq // tm)) == 0

        @pl.when(first_of_seq)
        def _():
            gbuf_ref[0:SUBLANES, :] = jnp.zeros((SUBLANES, tf), F32)

        @pl.when(jnp.logical_not(first_of_seq))
        def _():
            gbuf_ref[0:SUBLANES, :] = gtail_ref[j]

        gbuf_ref[SUBLANES:SUBLANES + tm, :] = gate
        gate_c = bf_ref[...] + gate * w_now
        for back in range(1, FFN_CONV_W):
            tap = FFN_CONV_W - 1 - back
            gate_c = gate_c + gbuf_ref[SUBLANES - back:SUBLANES - back + tm, :] * wf_ref[tap:tap + 1, :]
        tail = gate[tm - SUBLANES:tm, :]
        gtail_ref[j] = tail
        gate_ref[...] = tail
    hidden = (jax.nn.gelu(gate_c) * up).astype(BF16)
    acc_ref[...] += jnp.dot(hidden, wd_ref[...], preferred_element_type=F32)

    @pl.when(j == pl.num_programs(1) - 1)
    def _():
        x2 = _layer_norm(DEEPNORM_ALPHA * x1_ref[...] + acc_ref[...], g2_ref[...], b2_ref[...])
        emb = jnp.dot(pe_ref[...].astype(BF16), wple_ref[...], preferred_element_type=F32)
        gate_e = jax.nn.sigmoid(jnp.dot(x2.astype(BF16), wpleg_ref[...], preferred_element_type=F32) + bpleg_ref[...])
        o_ref[...] = _layer_norm(DEEPNORM_ALPHA * x2 + emb * gate_e, g3_ref[...], b3_ref[...])


def _ffn(x1, pe, wup, wdown, wf, bfc, g2, b2, wple, wpleg, bpleg, g3, b3, tm, tf, rows_per_seq, hist_t=None):
    t = x1.shape[0]
    d = D_MODEL
    n_chunks = D_FF // tf
    per_row_history = hist_t is not None
    row = pl.BlockSpec((tm, d), lambda i, j: (i, 0))
    vec = pl.BlockSpec((1, d), lambda i, j: (0, 0))
    fvec = pl.BlockSpec((1, tf), lambda i, j: (0, j))
    in_specs = [
        row,
        pl.BlockSpec((d, tf), lambda i, j: (0, j)),
        pl.BlockSpec((d, tf), lambda i, j: (0, n_chunks + j)),
        pl.BlockSpec((tf, d), lambda i, j: (j, 0)),
        pl.BlockSpec((FFN_CONV_W, tf), lambda i, j: (0, j)),
        fvec, vec, vec,
        pl.BlockSpec((tm, D_PLE), lambda i, j: (i, 0)),
        pl.BlockSpec((D_PLE, d), lambda i, j: (0, 0)),
        pl.BlockSpec((d, d), lambda i, j: (0, 0)),
        vec, vec, vec,
    ]
    args = [x1, wup, wup, wdown, wf, bfc, g2, b2, pe, wple, wpleg, bpleg, g3, b3]
    scratch = [pltpu.VMEM((tm, d), F32), pltpu.VMEM((tm, d), BF16)]
    if per_row_history:
        in_specs.append(pl.BlockSpec((FFN_CONV_W - 1, tm, tf), lambda i, j: (0, i, j)))
        args.append(hist_t)
        gate_spec = pl.BlockSpec((tm, tf), lambda i, j: (i, j))
        gate_shape = jax.ShapeDtypeStruct((t, D_FF), F32)
    else:
        gate_spec = pl.BlockSpec((None, SUBLANES, tf), lambda i, j: (i, 0, j))
        gate_shape = jax.ShapeDtypeStruct((t // tm, SUBLANES, D_FF), F32)
        scratch += [pltpu.VMEM((tm + SUBLANES, tf), F32), pltpu.VMEM((n_chunks, SUBLANES, tf), F32)]
    kernel = functools.partial(_ffn_kernel, tm=tm, tf=tf, rows_per_seq=rows_per_seq, per_row_history=per_row_history)
    return pl.pallas_call(
        kernel,
        grid=(t // tm, n_chunks),
        in_specs=in_specs,
        out_specs=[row, gate_spec],
        out_shape=[jax.ShapeDtypeStruct((t, d), F32), gate_shape],
        scratch_shapes=scratch,
        compiler_params=_compiler_params(("arbitrary", "arbitrary")),
        name="ffn",
    )(*args)


def kernel(x_prompt, x_sample, p_prompt, p_sample, cache_k, cache_v, page_table, state_rnn_conv, state_rnn_h, state_ffn_conv, w_in, sb_bias, w_rconv, b_rconv, w_ra, b_ra, w_ri, b_ri, lru_lambda, w_pa, w_pb, w_o, ln1_g, ln1_b, w_up, w_fconv, b_fconv, w_down, ln2_g, ln2_b, w_ple, w_pleg, b_pleg, ln3_g, ln3_b):
    assert w_in.shape[0] == DEPTH == 1
    bsz, seq, d = x_prompt.shape
    dec = x_sample.shape[0]
    assert x_sample.shape[1] == 1 and d == D_MODEL
    n_phys = cache_k.shape[1]

    row_vec = lambda a: a.reshape(1, -1)
    w_in_bf = w_in[0].astype(BF16)
    wra, wri = w_ra[0].astype(BF16), w_ri[0].astype(BF16)
    wpa, wpb, wo = w_pa[0].astype(BF16), w_pb[0].astype(BF16), w_o[0].astype(BF16)
    wup, wdown = w_up[0].astype(BF16), w_down[0].astype(BF16)
    wple, wpleg = w_ple[0].astype(BF16), w_pleg[0].astype(BF16)
    rnn_w = (w_rconv[0], row_vec(b_rconv[0]), wra, row_vec(b_ra[0]), wri, row_vec(b_ri[0]), row_vec(lru_lambda[0]))
    ffn_w = (wup, wdown, w_fconv[0], row_vec(b_fconv[0]), row_vec(ln2_g[0]), row_vec(ln2_b[0]),
             wple, wpleg, row_vec(b_pleg[0]), row_vec(ln3_g[0]), row_vec(ln3_b[0]))
    merge_w = (wpa, wpb, wo, row_vec(ln1_g[0]), row_vec(ln1_b[0]))

    xp = x_prompt.reshape(bsz * seq, d)
    wt_kv_bf = jnp.transpose(w_in[0][:, d:3 * d]).astype(BF16)
    q_p, kt_p, vt_p, kb_p, vb_p, rest_p = _inproj(xp, w_in_bf, tm=512, wt_kv_bf=wt_kv_bf, rows_per_seq=seq)
    bias2 = sb_bias[0] * LOG2E
    att_p = _sb_prompt(q_p, kb_p, vb_p, bias2, tq=256, pairs_per_step=4)
    rnn_p, h_p = _rnn_scan(rest_p, jnp.zeros((bsz, SUBLANES, d), F32), jnp.zeros((bsz, 1, d), F32), *rnn_w,
                           bsz=bsz, seq=seq, ts=256)
    x1_p = _merge(xp, att_p, rnn_p, rest_p, *merge_w, tm=512)
    ffn_tm = 512
    y_p, gate_tails = _ffn(x1_p, p_prompt[0].reshape(bsz * seq, D_PLE), *ffn_w, tm=ffn_tm, tf=512, rows_per_seq=seq)
    rest_p4 = rest_p.reshape(bsz, seq, 4 * d)
    rconv_p = rest_p4[:, seq - (RNN_CONV_W - 1):, :d]
    fconv_p = gate_tails.reshape(bsz, seq // ffn_tm, SUBLANES, D_FF)[:, -1, SUBLANES - (FFN_CONV_W - 1):, :]
    hd = (N_HEADS, HEAD_DIM)
    k_p = jnp.transpose(kt_p.reshape(bsz, *hd, seq), (0, 3, 1, 2))
    v_p = jnp.transpose(vt_p.reshape(bsz, *hd, seq), (0, 3, 1, 2))

    xs = x_sample.reshape(dec, d)
    q_s, k_s, v_s, rest_s = _inproj(xs, w_in_bf, tm=dec)
    bias_tile = jnp.broadcast_to(bias2[:, None], (N_HEADS, LANES))
    k_pool = jnp.transpose(cache_k[0], (0, 2, 3, 1)).reshape(n_phys, d, PAGE_SIZE)
    v_pool = jnp.transpose(cache_v[0], (0, 2, 3, 1)).reshape(n_phys, d, PAGE_SIZE)
    att_s = _sb_sample(q_s.reshape(dec, 1, d), k_s.reshape(dec, 1, d), v_s.reshape(dec, 1, d), bias_tile,
                       k_pool, v_pool, page_table, pages_per_step=8).reshape(dec, d)
    rnn_s, h_s = _rnn_step(rest_s, jnp.swapaxes(state_rnn_conv[0], 0, 1), state_rnn_h[0], *rnn_w)
    x1_s = _merge(xs, att_s, rnn_s, rest_s, *merge_w, tm=dec)
    y_s, gate_s = _ffn(x1_s, p_sample[0].reshape(dec, D_PLE), *ffn_w, tm=dec, tf=512, rows_per_seq=1,
                       hist_t=jnp.swapaxes(state_ffn_conv[0], 0, 1))
    rconv_s = jnp.concatenate([state_rnn_conv[0][:, 1:], rest_s[:, None, :d]], axis=1)
    fconv_s = jnp.concatenate([state_ffn_conv[0][:, 1:], gate_s[:, None, :]], axis=1)

    return (y_p.reshape(bsz, seq, d), y_s.reshape(dec, 1, d),
            k_p[None], v_p[None],
            rconv_p[None], h_p.reshape(1, bsz, d), fconv_p[None],
            k_s.reshape(1, dec, 1, *hd), v_s.reshape(1, dec, 1, *hd),
            rconv_s[None], h_s[None], fconv_s[None])
```

```python
import functools

import jax
import jax.numpy as jnp
from jax import lax
from jax.experimental import pallas as pl
from jax.experimental.pallas import tpu as pltpu

F32 = jnp.float32
BF16 = jnp.bfloat16

D_MODEL = 1024
N_HEADS = 16
HEAD_DIM = 64
N_RNN_BLOCKS = 8
RNN_BLOCK = D_MODEL // N_RNN_BLOCKS
RNN_CONV_W = 4
LRU_C = 8.0
D_FF = 3 * D_MODEL
FFN_CONV_W = 3
D_PLE = 256
PAGE_SIZE = 128
LN_EPS = 1e-5
DEPTH = 1
DEEPNORM_ALPHA = (2.0 * DEPTH) ** 0.25
N_IN_SPLITS = 7

LANES = 128
SUBLANES = 8
HEADS_PER_LANE_TILE = LANES // HEAD_DIM
KEY_BLOCK = 256
LOG2E = 1.4426950408889634
VMEM_LIMIT_BYTES = 56 * 1024 * 1024


def _compiler_params(semantics):
    return pltpu.CompilerParams(dimension_semantics=semantics, vmem_limit_bytes=VMEM_LIMIT_BYTES)


def _layer_norm(x, g, b):
    mu = jnp.mean(x, axis=-1, keepdims=True)
    xc = x - mu
    var = jnp.mean(xc * xc, axis=-1, keepdims=True)
    return xc * lax.rsqrt(var + LN_EPS) * g + b


def _softplus(z):
    return jnp.maximum(z, 0.0) + jnp.log(1.0 + jnp.exp(-jnp.abs(z)))


def _softplus2(z2):
    return jnp.maximum(z2, 0.0) + jnp.log2(1.0 + jnp.exp2(-jnp.abs(z2)))


def _neg_suffix_sum_matrix(n):
    j = lax.broadcasted_iota(jnp.int32, (n, n), 0)
    s = lax.broadcasted_iota(jnp.int32, (n, n), 1)
    return jnp.where(j >= s, -1.0, 0.0).astype(BF16)


def _inproj_kernel(*refs, tm, key_major):
    if key_major:
        x_ref, w_ref, wt_ref, q_ref, kf_ref, vf_ref, kb_ref, vb_ref, rest_ref, xb_ref = refs
    else:
        x_ref, w_ref, q_ref, kf_ref, vf_ref, rest_ref, xb_ref = refs
    j = pl.program_id(1)

    @pl.when(j == 0)
    def _():
        xb_ref[...] = x_ref[...].astype(BF16)

    def project():
        return jnp.dot(xb_ref[...], w_ref[...], preferred_element_type=F32)

    def project_key_major():
        return lax.dot_general(wt_ref[...], xb_ref[...], (((1,), (1,)), ((), ())), preferred_element_type=F32)

    @pl.when(j == 0)
    def _():
        q_ref[...] = (project() * (LOG2E * HEAD_DIM ** -0.5)).astype(BF16)

    @pl.when(j == 1)
    def _():
        if key_major:
            kt = project_key_major()
            kf_ref[...] = kt
            for kb in range(tm // KEY_BLOCK):
                kb_ref[kb] = kt[:, kb * KEY_BLOCK:(kb + 1) * KEY_BLOCK].astype(BF16)
        else:
            kf_ref[...] = project()

    @pl.when(j == 2)
    def _():
        if key_major:
            vf_ref[...] = project_key_major()
            vb_ref[...] = project().astype(BF16)
        else:
            vf_ref[...] = project()

    @pl.when(j >= 3)
    def _():
        rest_ref[...] = project()


def _inproj(x, w_in_bf, tm, wt_kv_bf=None, rows_per_seq=None):
    t = x.shape[0]
    d = D_MODEL
    key_major = wt_kv_bf is not None
    x_spec = pl.BlockSpec((tm, d), lambda i, j: (i, 0))
    q_spec = pl.BlockSpec((tm, d), lambda i, j: (i, 0))
    rest_spec = pl.BlockSpec((tm, d), lambda i, j: (i, jnp.maximum(j - 3, 0)))
    q_shape = jax.ShapeDtypeStruct((t, d), BF16)
    rest_shape = jax.ShapeDtypeStruct((t, 4 * d), F32)
    if key_major:
        tiles = rows_per_seq // tm
        bsz = t // rows_per_seq
        kb_per_tile = tm // KEY_BLOCK
        in_specs = [
            x_spec,
            pl.BlockSpec((d, d), lambda i, j: (0, jnp.where(j == 1, 0, j))),
            pl.BlockSpec((d, d), lambda i, j: (jnp.clip(j - 1, 0, 1), 0)),
        ]
        f_spec = pl.BlockSpec((None, d, tm), lambda i, j: (i // tiles, 0, i % tiles))
        kb_spec = pl.BlockSpec((None, kb_per_tile, d, KEY_BLOCK), lambda i, j: (i // tiles, i % tiles, 0, 0))
        f_shape = jax.ShapeDtypeStruct((bsz, d, rows_per_seq), F32)
        kb_shape = jax.ShapeDtypeStruct((bsz, rows_per_seq // KEY_BLOCK, d, KEY_BLOCK), BF16)
        out_specs = [q_spec, f_spec, f_spec, kb_spec, q_spec, rest_spec]
        out_shape = [q_shape, f_shape, f_shape, kb_shape, q_shape, rest_shape]
        args = (x, w_in_bf, wt_kv_bf)
    else:
        in_specs = [x_spec, pl.BlockSpec((d, d), lambda i, j: (0, j))]
        f_spec = pl.BlockSpec((tm, d), lambda i, j: (i, 0))
        f_shape = jax.ShapeDtypeStruct((t, d), F32)
        out_specs = [q_spec, f_spec, f_spec, rest_spec]
        out_shape = [q_shape, f_shape, f_shape, rest_shape]
        args = (x, w_in_bf)
    return pl.pallas_call(
        functools.partial(_inproj_kernel, tm=tm, key_major=key_major),
        grid=(t // tm, N_IN_SPLITS),
        in_specs=in_specs,
        out_specs=out_specs,
        out_shape=out_shape,
        scratch_shapes=[pltpu.VMEM((tm, d), BF16)],
        compiler_params=_compiler_params(("arbitrary", "arbitrary")),
        name="inproj",
    )(*args)


SAMPLE_PAGES_PER_GROUP = 4
SAMPLE_SLOTS = 3


def _sb_kernel(pt_ref, bias_ref, q_ref, k_ref, v_ref, qs_ref, knew_ref, vnew_ref, bias_tile_ref, kpool_ref, vpool_ref,
               o_ref, os_ref, acc_ref, rem_ref, kbuf_ref, vbuf_ref, sem_ref, acc_s_ref, rem_s_ref, it_ref,
               *, tq, pairs_per_step):
    group = pl.program_id(1)
    qi = pl.program_id(2)
    heads = range(HEADS_PER_LANE_TILE)
    pairs = range(pairs_per_step)
    kb = KEY_BLOCK
    col_head = lax.broadcasted_iota(jnp.int32, (1, HEADS_PER_LANE_TILE * kb), 1) // kb
    first_head = [HEADS_PER_LANE_TILE * (group * pairs_per_step + p) for p in pairs]
    bias_row = [jnp.where(col_head == 0, bias_ref[first_head[p]], bias_ref[first_head[p] + 1]) for p in pairs]
    k_row_head = lax.broadcasted_iota(jnp.int32, (LANES, kb), 0) // HEAD_DIM
    v_lane_head = lax.broadcasted_iota(jnp.int32, (kb, LANES), 1) // HEAD_DIM
    k_mask = [jnp.where(k_row_head == h, 1.0, 0.0).astype(BF16) for h in heads]
    v_mask = [jnp.where(v_lane_head == h, 1.0, 0.0).astype(BF16) for h in heads]
    nsm = _neg_suffix_sum_matrix(kb)
    q_all = q_ref[...]
    key_in_block = lax.broadcasted_iota(jnp.int32, (1, kb), 1)
    row_t = qi * tq + lax.broadcasted_iota(jnp.int32, (tq, 1), 0)

    n_seqs = qs_ref.shape[0]
    n_pages = pt_ref.shape[0] // n_seqs
    groups_per_seq = n_pages // SAMPLE_PAGES_PER_GROUP
    n_groups = n_seqs * groups_per_seq
    lookahead = SAMPLE_SLOTS - 1
    own = (lax.broadcasted_iota(jnp.int32, (N_HEADS, D_MODEL), 0)
           == lax.broadcasted_iota(jnp.int32, (N_HEADS, D_MODEL), 1) // HEAD_DIM)
    nsm_page = _neg_suffix_sum_matrix(PAGE_SIZE)

    def page_copies(g, slot):
        seq = g // groups_per_seq
        latest = (n_pages - 1) - (g % groups_per_seq) * SAMPLE_PAGES_PER_GROUP
        copies = []
        for p in range(SAMPLE_PAGES_PER_GROUP):
            phys = pt_ref[seq * n_pages + latest - p]
            copies.append(pltpu.make_async_copy(kpool_ref.at[phys], kbuf_ref.at[slot, p], sem_ref.at[slot, 2 * p]))
            copies.append(pltpu.make_async_copy(vpool_ref.at[phys], vbuf_ref.at[slot, p], sem_ref.at[slot, 2 * p + 1]))
        return copies

    first_step = (pl.program_id(0) == 0) & (group == 0) & (qi == 0)

    @pl.when(first_step)
    def _():
        it_ref[0] = 0
        acc_s_ref[...] = jnp.zeros_like(acc_s_ref)
        rem_s_ref[...] = jnp.zeros_like(rem_s_ref)
        for g in range(lookahead):
            for c in page_copies(g, g):
                c.start()

    def sample_group():
        it = it_ref[0]
        slot = it % SAMPLE_SLOTS
        live = it < n_groups

        @pl.when(live)
        def _():
            for c in page_copies(it, slot):
                c.wait()
            ahead = it + lookahead

            @pl.when(ahead < n_groups)
            def _():
                for c in page_copies(ahead, ahead % SAMPLE_SLOTS):
                    c.start()

        seq = jnp.minimum(it // groups_per_seq, n_seqs - 1)
        in_seq = it % groups_per_seq
        q_rows = jnp.where(own, qs_ref[seq].astype(F32), 0.0).astype(BF16)
        bias = bias_tile_ref[...]

        n_new = knew_ref.shape[1]
        assert n_new == 1
        key_idx = lax.broadcasted_iota(jnp.int32, (N_HEADS, n_new), 1)
        valid = key_idx < (n_new - 1)
        s_new = jnp.sum(q_rows.astype(F32) * knew_ref[seq], axis=-1, keepdims=True)
        z_new = s_new + bias[:, :1]
        sp_new = jnp.where(valid, _softplus2(z_new), 0.0)
        a_new = jnp.where(valid, jnp.exp2(z_new - sp_new), 0.0)
        starts_seq = in_seq == 0
        rem = jnp.where(starts_seq, jnp.broadcast_to(-sp_new, rem_s_ref.shape), rem_s_ref[...])
        acc = jnp.where(starts_seq, a_new * vnew_ref[seq], acc_s_ref[...])

        pages = range(SAMPLE_PAGES_PER_GROUP)
        zs = [jnp.dot(q_rows, kbuf_ref[slot, p].astype(BF16), preferred_element_type=F32) + bias for p in pages]
        sps = [_softplus2(z) for z in zs]
        css = [jnp.dot(sp.astype(BF16), nsm_page, preferred_element_type=F32) for sp in sps]
        tots = [jnp.sum(sp, axis=-1, keepdims=True) for sp in sps]
        for p in pages:
            a = jnp.exp2(zs[p] + css[p] + rem)
            rem = rem - tots[p]
            acc = acc + lax.dot_general(a.astype(BF16), vbuf_ref[slot, p].astype(BF16), (((1,), (1,)), ((), ())),
                                        preferred_element_type=F32)
        rem_s_ref[...] = rem
        acc_s_ref[...] = acc
        it_ref[0] = it + 1

        def finish():
            @pl.when(live & (in_seq == groups_per_seq - 1))
            def _():
                os_ref[seq] = jnp.sum(jnp.where(own, acc_s_ref[...], 0.0), axis=0, keepdims=True).astype(os_ref.dtype)

        return finish

    acc_ref[...] = jnp.zeros_like(acc_ref)
    rem_ref[...] = jnp.zeros_like(rem_ref)

    def sweep(j, masked):
        finish_sample_group = sample_group()
        k_all = k_ref[j]
        v_all = v_ref[pl.ds(pl.multiple_of(j * kb, kb), kb), :]
        if masked:
            valid = (j * kb + key_in_block) < row_t
            valid_pair = jnp.concatenate([valid] * HEADS_PER_LANE_TILE, axis=1)
        zs = []
        for p in pairs:
            k2t = k_all[p * LANES:(p + 1) * LANES]
            w = jnp.concatenate([k2t * k_mask[h] for h in heads], axis=1)
            zs.append(jnp.dot(q_all[:, p * LANES:(p + 1) * LANES], w, preferred_element_type=F32) + bias_row[p])
        sps = [_softplus2(z) for z in zs]
        if masked:
            sps = [jnp.where(valid_pair, sp, 0.0) for sp in sps]
        css = []
        for sp in sps:
            sp_rows = jnp.concatenate([sp[:, h * kb:(h + 1) * kb] for h in heads], axis=0).astype(BF16)
            css.append(jnp.dot(sp_rows, nsm, preferred_element_type=F32))
        for p in pairs:
            a_parts = []
            for h in heads:
                rem = rem_ref[HEADS_PER_LANE_TILE * p + h]
                arg = (zs[p][:, h * kb:(h + 1) * kb] + css[p][h * tq:(h + 1) * tq]
                       + jnp.tile(rem, (1, kb // LANES)))
                a_h = jnp.exp2(arg)
                if masked:
                    a_h = jnp.where(valid, a_h, 0.0)
                a_parts.append(a_h.astype(BF16))
                rem_ref[HEADS_PER_LANE_TILE * p + h] = rem - jnp.sum(
                    sps[p][:, h * kb:(h + 1) * kb], axis=-1, keepdims=True)
            a = jnp.concatenate(a_parts, axis=1)
            v2 = v_all[:, p * LANES:(p + 1) * LANES]
            v_bd = jnp.concatenate([v2 * v_mask[h] for h in heads], axis=0)
            acc_ref[:, p * LANES:(p + 1) * LANES] += jnp.dot(a, v_bd, preferred_element_type=F32)
        finish_sample_group()

    blocks_per_q = tq // kb
    for d in range(blocks_per_q):
        sweep(qi * blocks_per_q + (blocks_per_q - 1 - d), True)

    n_full = qi * blocks_per_q

    def body(jj, carry):
        sweep(n_full - 1 - jj, False)
        return carry

    lax.fori_loop(0, n_full, body, 0)
    o_ref[...] = acc_ref[...].astype(o_ref.dtype)


def _sb_attention(q, kb, vb, bias2, qs3, k_new3, v_new3, k_pool, v_pool, page_table, tq, pairs_per_step):
    bsz, n_kb = kb.shape[:2]
    seq = n_kb * KEY_BLOCK
    width = pairs_per_step * LANES
    groups = D_MODEL // width
    rows_q = seq // tq
    dec, n_pages = page_table.shape
    blocks_per_q = tq // KEY_BLOCK
    n_sweeps = bsz * groups * blocks_per_q * rows_q * (rows_q + 1) // 2
    assert n_pages % SAMPLE_PAGES_PER_GROUP == 0 and n_sweeps >= dec * n_pages // SAMPLE_PAGES_PER_GROUP
    bias_tile = jnp.broadcast_to(bias2[:, None], (N_HEADS, LANES))
    whole = lambda shape: pl.BlockSpec(shape, lambda b, g, i, pt: (0,) * len(shape))
    page_buf = pltpu.VMEM((SAMPLE_SLOTS, SAMPLE_PAGES_PER_GROUP, D_MODEL, PAGE_SIZE), F32)
    grid_spec = pltpu.PrefetchScalarGridSpec(
        num_scalar_prefetch=1,
        grid=(bsz, groups, rows_q),
        in_specs=[
            pl.BlockSpec(memory_space=pltpu.SMEM),
            pl.BlockSpec((tq, width), lambda b, g, i, pt: (b * rows_q + i, g)),
            pl.BlockSpec((None, n_kb, width, KEY_BLOCK), lambda b, g, i, pt: (b, 0, g, 0)),
            pl.BlockSpec((seq, width), lambda b, g, i, pt: (b, g)),
            whole((dec, 1, D_MODEL)), whole((dec, 1, D_MODEL)), whole((dec, 1, D_MODEL)),
            whole((N_HEADS, LANES)),
            pl.BlockSpec(memory_space=pl.ANY), pl.BlockSpec(memory_space=pl.ANY),
        ],
        out_specs=[
            pl.BlockSpec((tq, width), lambda b, g, i, pt: (b * rows_q + i, g)),
            whole((dec, 1, D_MODEL)),
        ],
        scratch_shapes=[
            pltpu.VMEM((tq, width), F32),
            pltpu.VMEM((pairs_per_step * HEADS_PER_LANE_TILE, tq, LANES), F32),
            page_buf, page_buf,
            pltpu.SemaphoreType.DMA((SAMPLE_SLOTS, 2 * SAMPLE_PAGES_PER_GROUP)),
            pltpu.VMEM((N_HEADS, D_MODEL), F32),
            pltpu.VMEM((N_HEADS, PAGE_SIZE), F32),
            pltpu.SMEM((1,), jnp.int32),
        ],
    )
    return pl.pallas_call(
        functools.partial(_sb_kernel, tq=tq, pairs_per_step=pairs_per_step),
        grid_spec=grid_spec,
        out_shape=[jax.ShapeDtypeStruct((bsz * seq, D_MODEL), BF16), jax.ShapeDtypeStruct((dec, 1, D_MODEL), BF16)],
        compiler_params=_compiler_params(("arbitrary", "arbitrary", "arbitrary")),
        name="sb_attention",
    )(page_table.reshape(-1), bias2, q, kb, vb, qs3, k_new3, v_new3, bias_tile, k_pool, v_pool)


def _rglru_gates(xc, wra_ref, bra_ref, wri_ref, bri_ref, lam_ref):
    xb = xc.astype(BF16)
    r_parts = []
    i_parts = []
    for n in range(N_RNN_BLOCKS):
        xn = xb[:, n * RNN_BLOCK:(n + 1) * RNN_BLOCK]
        r_parts.append(jnp.dot(xn, wra_ref[n], preferred_element_type=F32))
        i_parts.append(jnp.dot(xn, wri_ref[n], preferred_element_type=F32))
    r = jax.nn.sigmoid(jnp.concatenate(r_parts, axis=-1) + bra_ref[...])
    i = jax.nn.sigmoid(jnp.concatenate(i_parts, axis=-1) + bri_ref[...])
    log_a = (-LRU_C * r) * _softplus(-lam_ref[...])
    a = jnp.exp(log_a)
    u = (xc * i) * jnp.sqrt(jnp.tanh(-log_a) * (a * a + 1.0))
    return a, u


def _rnn_scan_kernel(xr_ref, gr_ref, hist_ref, h0_ref, wc_ref, bc_ref, wra_ref, bra_ref, wri_ref, bri_ref,
                     lam_ref, out_ref, hlast_ref, xbuf_ref, a_ref, u_ref, h_ref, *, ts):
    st = pl.program_id(1)

    @pl.when(st == 0)
    def _():
        xbuf_ref[0:SUBLANES, :] = hist_ref[...]
        h_ref[...] = h0_ref[...]

    xbuf_ref[SUBLANES:SUBLANES + ts, :] = xr_ref[...]
    xc = bc_ref[...] + xbuf_ref[SUBLANES:SUBLANES + ts, :] * wc_ref[RNN_CONV_W - 1:RNN_CONV_W, :]
    for back in range(1, RNN_CONV_W):
        tap = RNN_CONV_W - 1 - back
        xc = xc + xbuf_ref[SUBLANES - back:SUBLANES - back + ts, :] * wc_ref[tap:tap + 1, :]
    xbuf_ref[0:SUBLANES, :] = xbuf_ref[ts:ts + SUBLANES, :]

    a, u = _rglru_gates(xc, wra_ref, bra_ref, wri_ref, bri_ref, lam_ref)
    a_ref[...] = a
    u_ref[...] = u

    def row(t, h):
        h = a_ref[pl.ds(t, 1), :] * h + u_ref[pl.ds(t, 1), :]
        u_ref[pl.ds(t, 1), :] = h
        return h

    h_last = lax.fori_loop(0, ts, row, h_ref[...], unroll=8)
    h_ref[...] = h_last
    hlast_ref[...] = h_last
    out_ref[...] = (jax.nn.gelu(gr_ref[...]) * u_ref[...]).astype(out_ref.dtype)


def _rnn_scan(rest, hist8, h0, wc, bc, wra, bra, wri, bri, lam, bsz, seq, ts):
    d = D_MODEL
    tiles = seq // ts
    vec = pl.BlockSpec((1, d), lambda b, s: (0, 0))
    blk = pl.BlockSpec((N_RNN_BLOCKS, RNN_BLOCK, RNN_BLOCK), lambda b, s: (0, 0, 0))
    kernel = functools.partial(_rnn_scan_kernel, ts=ts)
    return pl.pallas_call(
        kernel,
        grid=(bsz, tiles),
        in_specs=[
            pl.BlockSpec((ts, d), lambda b, s: (b * tiles + s, 0)),
            pl.BlockSpec((ts, d), lambda b, s: (b * tiles + s, 1)),
            pl.BlockSpec((None, SUBLANES, d), lambda b, s: (b, 0, 0)),
            pl.BlockSpec((None, 1, d), lambda b, s: (b, 0, 0)),
            pl.BlockSpec((RNN_CONV_W, d), lambda b, s: (0, 0)),
            vec, blk, vec, blk, vec, vec,
        ],
        out_specs=[
            pl.BlockSpec((ts, d), lambda b, s: (b * tiles + s, 0)),
            pl.BlockSpec((None, 1, d), lambda b, s: (b, 0, 0)),
        ],
        out_shape=[
            jax.ShapeDtypeStruct((bsz * seq, d), BF16),
            jax.ShapeDtypeStruct((bsz, 1, d), F32),
        ],
        scratch_shapes=[
            pltpu.VMEM((ts + SUBLANES, d), F32),
            pltpu.VMEM((ts, d), F32),
            pltpu.VMEM((ts, d), F32),
            pltpu.VMEM((1, d), F32),
        ],
        compiler_params=_compiler_params(("arbitrary", "arbitrary")),
        name="rnn_scan",
    )(rest, rest, hist8, h0, wc, bc, wra, bra, wri, bri, lam)


def _rnn_step_kernel(xr_ref, gr_ref, hist_ref, h0_ref, wc_ref, bc_ref, wra_ref, bra_ref, wri_ref, bri_ref,
                     lam_ref, out_ref, h_ref):
    xc = bc_ref[...] + xr_ref[...] * wc_ref[RNN_CONV_W - 1:RNN_CONV_W, :]
    for tap in range(RNN_CONV_W - 1):
        xc = xc + hist_ref[tap] * wc_ref[tap:tap + 1, :]
    a, u = _rglru_gates(xc, wra_ref, bra_ref, wri_ref, bri_ref, lam_ref)
    h = a * h0_ref[...] + u
    h_ref[...] = h
    out_ref[...] = (jax.nn.gelu(gr_ref[...]) * h).astype(out_ref.dtype)


def _rnn_step(rest, hist_t, h0, wc, bc, wra, bra, wri, bri, lam):
    bsz = rest.shape[0]
    d = D_MODEL
    vec = pl.BlockSpec((1, d), lambda i: (0, 0))
    blk = pl.BlockSpec((N_RNN_BLOCKS, RNN_BLOCK, RNN_BLOCK), lambda i: (0, 0, 0))
    return pl.pallas_call(
        _rnn_step_kernel,
        grid=(1,),
        in_specs=[
            pl.BlockSpec((bsz, d), lambda i: (0, 0)),
            pl.BlockSpec((bsz, d), lambda i: (0, 1)),
            pl.BlockSpec((RNN_CONV_W - 1, bsz, d), lambda i: (0, 0, 0)),
            pl.BlockSpec((bsz, d), lambda i: (0, 0)),
            pl.BlockSpec((RNN_CONV_W, d), lambda i: (0, 0)),
            vec, blk, vec, blk, vec, vec,
        ],
        out_specs=[pl.BlockSpec((bsz, d), lambda i: (0, 0)), pl.BlockSpec((bsz, d), lambda i: (0, 0))],
        out_shape=[jax.ShapeDtypeStruct((bsz, d), BF16), jax.ShapeDtypeStruct((bsz, d), F32)],
        compiler_params=_compiler_params(("arbitrary",)),
        name="rnn_step",
    )(rest, rest, hist_t, h0, wc, bc, wra, bra, wri, bri, lam)


def _merge_kernel(x_ref, att_ref, rnn_ref, ga_ref, gb_ref, wpa_ref, wpb_ref, wo_ref, g_ref, b_ref, o_ref):
    pa = jnp.dot(att_ref[...], wpa_ref[...], preferred_element_type=F32)
    pb = jnp.dot(rnn_ref[...], wpb_ref[...], preferred_element_type=F32)
    merged = jax.nn.sigmoid(ga_ref[...]) * pa + jax.nn.sigmoid(gb_ref[...]) * pb
    y = jnp.dot(merged.astype(BF16), wo_ref[...], preferred_element_type=F32)
    o_ref[...] = _layer_norm(DEEPNORM_ALPHA * x_ref[...] + y, g_ref[...], b_ref[...])


def _merge(x, att, rnn, rest, wpa, wpb, wo, g, b, tm):
    t = x.shape[0]
    d = D_MODEL
    row = pl.BlockSpec((tm, d), lambda i: (i, 0))
    mat = pl.BlockSpec((d, d), lambda i: (0, 0))
    vec = pl.BlockSpec((1, d), lambda i: (0, 0))
    return pl.pallas_call(
        _merge_kernel,
        grid=(t // tm,),
        in_specs=[row, row, row,
                  pl.BlockSpec((tm, d), lambda i: (i, 2)), pl.BlockSpec((tm, d), lambda i: (i, 3)),
                  mat, mat, mat, vec, vec],
        out_specs=row,
        out_shape=jax.ShapeDtypeStruct((t, d), F32),
        compiler_params=_compiler_params(("arbitrary",)),
        name="merge",
    )(x, att, rnn, rest, rest, wpa, wpb, wo, g, b)


def _ffn_kernel(*refs, tm, tf, rows_per_seq, per_row_history):
    if per_row_history:
        (x1_ref, wu_ref, wg_ref, wd_ref, wf_ref, bf_ref, g2_ref, b2_ref, pe_ref, wple_ref, wpleg_ref, bpleg_ref,
         g3_ref, b3_ref, hist_ref, o_ref, gate_ref, acc_ref, xb_ref) = refs
    else:
        (x1_ref, wu_ref, wg_ref, wd_ref, wf_ref, bf_ref, g2_ref, b2_ref, pe_ref, wple_ref, wpleg_ref, bpleg_ref,
         g3_ref, b3_ref, o_ref, gate_ref, acc_ref, xb_ref, gbuf_ref, gtail_ref) = refs
    i = pl.program_id(0)
    j = pl.program_id(1)

    @pl.when(j == 0)
    def _():
        xb_ref[...] = x1_ref[...].astype(BF16)
        acc_ref[...] = jnp.zeros_like(acc_ref)

    xb = xb_ref[...]
    up = jnp.dot(xb, wu_ref[...], preferred_element_type=F32)
    gate = jnp.dot(xb, wg_ref[...], preferred_element_type=F32)
    w_now = wf_ref[FFN_CONV_W - 1:FFN_CONV_W, :]
    if per_row_history:
        gate_ref[...] = gate
        gate_c = bf_ref[...] + gate * w_now
        for tap in range(FFN_CONV_W - 1):
            gate_c = gate_c + hist_ref[tap] * wf_ref[tap:tap + 1, :]
    else:
        first_of_seq = (i % (rows_per_seq // tm)) == 0

        @pl.when(first_of_seq)
        def _():
            gbuf_ref[0:SUBLANES, :] = jnp.zeros((SUBLANES, tf), F32)

        @pl.when(jnp.logical_not(first_of_seq))
        def _():
            gbuf_ref[0:SUBLANES, :] = gtail_ref[j]

        gbuf_ref[SUBLANES:SUBLANES + tm, :] = gate
        gate_c = bf_ref[...] + gate * w_now
        for back in range(1, FFN_CONV_W):
            tap = FFN_CONV_W - 1 - back
            gate_c = gate_c + gbuf_ref[SUBLANES - back:SUBLANES - back + tm, :] * wf_ref[tap:tap + 1, :]
        tail = gate[tm - SUBLANES:tm, :]
        gtail_ref[j] = tail
        gate_ref[...] = tail
    hidden = (jax.nn.gelu(gate_c) * up).astype(BF16)
    acc_ref[...] += jnp.dot(hidden, wd_ref[...], preferred_element_type=F32)

    @pl.when(j == pl.num_programs(1) - 1)
    def _():
        x2 = _layer_norm(DEEPNORM_ALPHA * x1_ref[...] + acc_ref[...], g2_ref[...], b2_ref[...])
        emb = jnp.dot(pe_ref[...].astype(BF16), wple_ref[...], preferred_element_type=F32)
        gate_e = jax.nn.sigmoid(jnp.dot(x2.astype(BF16), wpleg_ref[...], preferred_element_type=F32) + bpleg_ref[...])
        o_ref[...] = _layer_norm(DEEPNORM_ALPHA * x2 + emb * gate_e, g3_ref[...], b3_ref[...])


def _ffn(x1, pe, wup, wdown, wf, bfc, g2, b2, wple, wpleg, bpleg, g3, b3, tm, tf, rows_per_seq, hist_t=None):
    t = x1.shape[0]
    d = D_MODEL
    n_chunks = D_FF // tf
    per_row_history = hist_t is not None
    row = pl.BlockSpec((tm, d), lambda i, j: (i, 0))
    vec = pl.BlockSpec((1, d), lambda i, j: (0, 0))
    fvec = pl.BlockSpec((1, tf), lambda i, j: (0, j))
    in_specs = [
        row,
        pl.BlockSpec((d, tf), lambda i, j: (0, j)),
        pl.BlockSpec((d, tf), lambda i, j: (0, n_chunks + j)),
        pl.BlockSpec((tf, d), lambda i, j: (j, 0)),
        pl.BlockSpec((FFN_CONV_W, tf), lambda i, j: (0, j)),
        fvec, vec, vec,
        pl.BlockSpec((tm, D_PLE), lambda i, j: (i, 0)),
        pl.BlockSpec((D_PLE, d), lambda i, j: (0, 0)),
        pl.BlockSpec((d, d), lambda i, j: (0, 0)),
        vec, vec, vec,
    ]
    args = [x1, wup, wup, wdown, wf, bfc, g2, b2, pe, wple, wpleg, bpleg, g3, b3]
    scratch = [pltpu.VMEM((tm, d), F32), pltpu.VMEM((tm, d), BF16)]
    if per_row_history:
        in_specs.append(pl.BlockSpec((FFN_CONV_W - 1, tm, tf), lambda i, j: (0, i, j)))
        args.append(hist_t)
        gate_spec = pl.BlockSpec((tm, tf), lambda i, j: (i, j))
        gate_shape = jax.ShapeDtypeStruct((t, D_FF), F32)
    else:
        gate_spec = pl.BlockSpec((None, SUBLANES, tf), lambda i, j: (i, 0, j))
        gate_shape = jax.ShapeDtypeStruct((t // tm, SUBLANES, D_FF), F32)
        scratch += [pltpu.VMEM((tm + SUBLANES, tf), F32), pltpu.VMEM((n_chunks, SUBLANES, tf), F32)]
    kernel = functools.partial(_ffn_kernel, tm=tm, tf=tf, rows_per_seq=rows_per_seq, per_row_history=per_row_history)
    return pl.pallas_call(
        kernel,
        grid=(t // tm, n_chunks),
        in_specs=in_specs,
        out_specs=[row, gate_spec],
        out_shape=[jax.ShapeDtypeStruct((t, d), F32), gate_shape],
        scratch_shapes=scratch,
        compiler_params=_compiler_params(("arbitrary", "arbitrary")),
        name="ffn",
    )(*args)


def kernel(x_prompt, x_sample, p_prompt, p_sample, cache_k, cache_v, page_table, state_rnn_conv, state_rnn_h, state_ffn_conv, w_in, sb_bias, w_rconv, b_rconv, w_ra, b_ra, w_ri, b_ri, lru_lambda, w_pa, w_pb, w_o, ln1_g, ln1_b, w_up, w_fconv, b_fconv, w_down, ln2_g, ln2_b, w_ple, w_pleg, b_pleg, ln3_g, ln3_b):
    assert w_in.shape[0] == DEPTH == 1
    bsz, seq, d = x_prompt.shape
    dec = x_sample.shape[0]
    assert x_sample.shape[1] == 1 and d == D_MODEL
    n_phys = cache_k.shape[1]

    row_vec = lambda a: a.reshape(1, -1)
    w_in_bf = w_in[0].astype(BF16)
    wra, wri = w_ra[0].astype(BF16), w_ri[0].astype(BF16)
    wpa, wpb, wo = w_pa[0].astype(BF16), w_pb[0].astype(BF16), w_o[0].astype(BF16)
    wup, wdown = w_up[0].astype(BF16), w_down[0].astype(BF16)
    wple, wpleg = w_ple[0].astype(BF16), w_pleg[0].astype(BF16)
    rnn_w = (w_rconv[0], row_vec(b_rconv[0]), wra, row_vec(b_ra[0]), wri, row_vec(b_ri[0]), row_vec(lru_lambda[0]))
    ffn_w = (wup, wdown, w_fconv[0], row_vec(b_fconv[0]), row_vec(ln2_g[0]), row_vec(ln2_b[0]),
             wple, wpleg, row_vec(b_pleg[0]), row_vec(ln3_g[0]), row_vec(ln3_b[0]))
    merge_w = (wpa, wpb, wo, row_vec(ln1_g[0]), row_vec(ln1_b[0]))

    xp = x_prompt.reshape(bsz * seq, d)
    xs = x_sample.reshape(dec, d)
    wt_kv_bf = jnp.transpose(w_in[0][:, d:3 * d]).astype(BF16)
    q_p, kt_p, vt_p, kb_p, vb_p, rest_p = _inproj(xp, w_in_bf, tm=512, wt_kv_bf=wt_kv_bf, rows_per_seq=seq)
    q_s, k_s, v_s, rest_s = _inproj(xs, w_in_bf, tm=dec)

    bias2 = sb_bias[0] * LOG2E
    k_pool = jnp.transpose(cache_k[0], (0, 2, 3, 1)).reshape(n_phys, d, PAGE_SIZE)
    v_pool = jnp.transpose(cache_v[0], (0, 2, 3, 1)).reshape(n_phys, d, PAGE_SIZE)
    att_p, att_s = _sb_attention(q_p, kb_p, vb_p, bias2, q_s.reshape(dec, 1, d), k_s.reshape(dec, 1, d),
                                 v_s.reshape(dec, 1, d), k_pool, v_pool, page_table, tq=256, pairs_per_step=4)
    att_s = att_s.reshape(dec, d)

    rnn_p, h_p = _rnn_scan(rest_p, jnp.zeros((bsz, SUBLANES, d), F32), jnp.zeros((bsz, 1, d), F32), *rnn_w,
                           bsz=bsz, seq=seq, ts=256)
    x1_p = _merge(xp, att_p, rnn_p, rest_p, *merge_w, tm=512)
    ffn_tm = 512
    y_p, gate_tails = _ffn(x1_p, p_prompt[0].reshape(bsz * seq, D_PLE), *ffn_w, tm=ffn_tm, tf=512, rows_per_seq=seq)
    rest_p4 = rest_p.reshape(bsz, seq, 4 * d)
    rconv_p = rest_p4[:, seq - (RNN_CONV_W - 1):, :d]
    fconv_p = gate_tails.reshape(bsz, seq // ffn_tm, SUBLANES, D_FF)[:, -1, SUBLANES - (FFN_CONV_W - 1):, :]
    hd = (N_HEADS, HEAD_DIM)
    k_p = jnp.transpose(kt_p.reshape(bsz, *hd, seq), (0, 3, 1, 2))
    v_p = jnp.transpose(vt_p.reshape(bsz, *hd, seq), (0, 3, 1, 2))

    rnn_s, h_s = _rnn_step(rest_s, jnp.swapaxes(state_rnn_conv[0], 0, 1), state_rnn_h[0], *rnn_w)
    x1_s = _merge(xs, att_s, rnn_s, rest_s, *merge_w, tm=dec)
    y_s, gate_s = _ffn(x1_s, p_sample[0].reshape(dec, D_PLE), *ffn_w, tm=dec, tf=512, rows_per_seq=1,
                       hist_t=jnp.swapaxes(state_ffn_conv[0], 0, 1))
    rconv_s = jnp.concatenate([state_rnn_conv[0][:, 1:], rest_s[:, None, :d]], axis=1)
    fconv_s = jnp.concatenate([state_ffn_conv[0][:, 1:], gate_s[:, None, :]], axis=1)

    return (y_p.reshape(bsz, seq, d), y_s.reshape(dec, 1, d),
            k_p[None], v_p[None],
            rconv_p[None], h_p.reshape(1, bsz, d), fconv_p[None],
            k_s.reshape(1, dec, 1, *hd), v_s.reshape(1, dec, 1, *hd),
            rconv_s[None], h_s[None], fconv_s[None])
```

```python
import functools

import jax
import jax.numpy as jnp
from jax import lax
from jax.experimental import pallas as pl
from jax.experimental.pallas import tpu as pltpu

F32 = jnp.float32
BF16 = jnp.bfloat16

D_MODEL = 1024
N_HEADS = 16
HEAD_DIM = 64
N_RNN_BLOCKS = 8
RNN_BLOCK = D_MODEL // N_RNN_BLOCKS
RNN_CONV_W = 4
LRU_C = 8.0
D_FF = 3 * D_MODEL
FFN_CONV_W = 3
D_PLE = 256
PAGE_SIZE = 128
LN_EPS = 1e-5
DEPTH = 1
DEEPNORM_ALPHA = (2.0 * DEPTH) ** 0.25
N_IN_SPLITS = 7

LANES = 128
SUBLANES = 8
HEADS_PER_LANE_TILE = LANES // HEAD_DIM
FFN_ROW_CHUNK = 64
FFN_PART_ROWS = 256
KEY_BLOCK = 256
LOG2E = 1.4426950408889634
VMEM_LIMIT_BYTES = 56 * 1024 * 1024


def _compiler_params(semantics):
    return pltpu.CompilerParams(dimension_semantics=semantics, vmem_limit_bytes=VMEM_LIMIT_BYTES)


def _layer_norm(x, g, b):
    mu = jnp.mean(x, axis=-1, keepdims=True)
    xc = x - mu
    var = jnp.mean(xc * xc, axis=-1, keepdims=True)
    return xc * lax.rsqrt(var + LN_EPS) * g + b


def _sigmoid(x):
    return 0.5 * jnp.tanh(0.5 * x) + 0.5


def _softplus(z):
    return jnp.maximum(z, 0.0) + jnp.log(1.0 + jnp.exp(-jnp.abs(z)))


def _softplus2(z2):
    return jnp.maximum(z2, 0.0) + jnp.log2(1.0 + jnp.exp2(-jnp.abs(z2)))


def _neg_suffix_sum_matrix(n):
    j = lax.broadcasted_iota(jnp.int32, (n, n), 0)
    s = lax.broadcasted_iota(jnp.int32, (n, n), 1)
    return jnp.where(j >= s, -1.0, 0.0).astype(BF16)


def _inproj_kernel(*refs, tm, key_major):
    if key_major:
        x_ref, w_ref, wt_ref, q_ref, kf_ref, vf_ref, kb_ref, vb_ref, rest_ref, xb_ref = refs
    else:
        x_ref, w_ref, q_ref, kf_ref, vf_ref, rest_ref, xb_ref = refs
    j = pl.program_id(1)

    @pl.when(j == 0)
    def _():
        xb_ref[...] = x_ref[...].astype(BF16)

    def project():
        return jnp.dot(xb_ref[...], w_ref[...], preferred_element_type=F32)

    def project_key_major():
        return lax.dot_general(wt_ref[...], xb_ref[...], (((1,), (1,)), ((), ())), preferred_element_type=F32)

    @pl.when(j == 0)
    def _():
        q_ref[...] = (project() * (LOG2E * HEAD_DIM ** -0.5)).astype(BF16)

    @pl.when(j == 1)
    def _():
        if key_major:
            kt = project_key_major()
            kf_ref[...] = kt
            for kb in range(tm // KEY_BLOCK):
                kb_ref[kb] = kt[:, kb * KEY_BLOCK:(kb + 1) * KEY_BLOCK].astype(BF16)
        else:
            kf_ref[...] = project()

    @pl.when(j == 2)
    def _():
        if key_major:
            vf_ref[...] = project_key_major()
            vb_ref[...] = project().astype(BF16)
        else:
            vf_ref[...] = project()

    @pl.when(j >= 3)
    def _():
        rest_ref[...] = project()


def _inproj(x, w_in_bf, tm, wt_kv_bf=None, rows_per_seq=None):
    t = x.shape[0]
    d = D_MODEL
    key_major = wt_kv_bf is not None
    x_spec = pl.BlockSpec((tm, d), lambda i, j: (i, 0))
    q_spec = pl.BlockSpec((tm, d), lambda i, j: (i, 0))
    rest_spec = pl.BlockSpec((tm, d), lambda i, j: (i, jnp.maximum(j - 3, 0)))
    q_shape = jax.ShapeDtypeStruct((t, d), BF16)
    rest_shape = jax.ShapeDtypeStruct((t, 4 * d), F32)
    if key_major:
        tiles = rows_per_seq // tm
        bsz = t // rows_per_seq
        kb_per_tile = tm // KEY_BLOCK
        in_specs = [
            x_spec,
            pl.BlockSpec((d, d), lambda i, j: (0, jnp.where(j == 1, 0, j))),
            pl.BlockSpec((d, d), lambda i, j: (jnp.clip(j - 1, 0, 1), 0)),
        ]
        f_spec = pl.BlockSpec((None, d, tm), lambda i, j: (i // tiles, 0, i % tiles))
        kb_spec = pl.BlockSpec((None, kb_per_tile, d, KEY_BLOCK), lambda i, j: (i // tiles, i % tiles, 0, 0))
        f_shape = jax.ShapeDtypeStruct((bsz, d, rows_per_seq), F32)
        kb_shape = jax.ShapeDtypeStruct((bsz, rows_per_seq // KEY_BLOCK, d, KEY_BLOCK), BF16)
        out_specs = [q_spec, f_spec, f_spec, kb_spec, q_spec, rest_spec]
        out_shape = [q_shape, f_shape, f_shape, kb_shape, q_shape, rest_shape]
        args = (x, w_in_bf, wt_kv_bf)
    else:
        in_specs = [x_spec, pl.BlockSpec((d, d), lambda i, j: (0, j))]
        f_spec = pl.BlockSpec((tm, d), lambda i, j: (i, 0))
        f_shape = jax.ShapeDtypeStruct((t, d), F32)
        out_specs = [q_spec, f_spec, f_spec, rest_spec]
        out_shape = [q_shape, f_shape, f_shape, rest_shape]
        args = (x, w_in_bf)
    return pl.pallas_call(
        functools.partial(_inproj_kernel, tm=tm, key_major=key_major),
        grid=(t // tm, N_IN_SPLITS),
        in_specs=in_specs,
        out_specs=out_specs,
        out_shape=out_shape,
        scratch_shapes=[pltpu.VMEM((tm, d), BF16)],
        compiler_params=_compiler_params(("arbitrary", "arbitrary")),
        name="inproj",
    )(*args)


SAMPLE_PAGES_PER_GROUP = 4
SAMPLE_SLOTS = 3


def _sb_kernel(pt_ref, bias_ref, q_ref, k_ref, v_ref, qs_ref, knew_ref, vnew_ref, bias_tile_ref, kpool_ref, vpool_ref,
               o_ref, os_ref, acc_ref, rem_ref, kbuf_ref, vbuf_ref, sem_ref, acc_s_ref, rem_s_ref, it_ref,
               *, tq, pairs_per_step):
    group = pl.program_id(1)
    qi = pl.program_id(2)
    heads = range(HEADS_PER_LANE_TILE)
    pairs = range(pairs_per_step)
    kb = KEY_BLOCK
    col_head = lax.broadcasted_iota(jnp.int32, (1, HEADS_PER_LANE_TILE * kb), 1) // kb
    first_head = [HEADS_PER_LANE_TILE * (group * pairs_per_step + p) for p in pairs]
    bias_row = [jnp.where(col_head == 0, bias_ref[first_head[p]], bias_ref[first_head[p] + 1]) for p in pairs]
    k_row_head = lax.broadcasted_iota(jnp.int32, (LANES, kb), 0) // HEAD_DIM
    v_lane_head = lax.broadcasted_iota(jnp.int32, (kb, LANES), 1) // HEAD_DIM
    k_mask = [jnp.where(k_row_head == h, 1.0, 0.0).astype(BF16) for h in heads]
    v_mask = [jnp.where(v_lane_head == h, 1.0, 0.0).astype(BF16) for h in heads]
    nsm = _neg_suffix_sum_matrix(kb)
    q_all = q_ref[...]
    key_in_block = lax.broadcasted_iota(jnp.int32, (1, kb), 1)
    row_t = qi * tq + lax.broadcasted_iota(jnp.int32, (tq, 1), 0)

    n_seqs = qs_ref.shape[0]
    n_pages = pt_ref.shape[0] // n_seqs
    groups_per_seq = n_pages // SAMPLE_PAGES_PER_GROUP
    n_groups = n_seqs * groups_per_seq
    lookahead = SAMPLE_SLOTS - 1
    own = (lax.broadcasted_iota(jnp.int32, (N_HEADS, D_MODEL), 0)
           == lax.broadcasted_iota(jnp.int32, (N_HEADS, D_MODEL), 1) // HEAD_DIM)
    nsm_page = _neg_suffix_sum_matrix(PAGE_SIZE)

    def page_copies(g, slot):
        seq = g // groups_per_seq
        latest = (n_pages - 1) - (g % groups_per_seq) * SAMPLE_PAGES_PER_GROUP
        copies = []
        for p in range(SAMPLE_PAGES_PER_GROUP):
            phys = pt_ref[seq * n_pages + latest - p]
            copies.append(pltpu.make_async_copy(kpool_ref.at[phys], kbuf_ref.at[slot, p], sem_ref.at[slot, 2 * p]))
            copies.append(pltpu.make_async_copy(vpool_ref.at[phys], vbuf_ref.at[slot, p], sem_ref.at[slot, 2 * p + 1]))
        return copies

    first_step = (pl.program_id(0) == 0) & (group == 0) & (qi == 0)

    @pl.when(first_step)
    def _():
        it_ref[0] = 0
        acc_s_ref[...] = jnp.zeros_like(acc_s_ref)
        rem_s_ref[...] = jnp.zeros_like(rem_s_ref)
        for g in range(lookahead):
            for c in page_copies(g, g):
                c.start()

    def sample_group():
        it = it_ref[0]
        slot = it % SAMPLE_SLOTS
        live = it < n_groups

        @pl.when(live)
        def _():
            for c in page_copies(it, slot):
                c.wait()
            ahead = it + lookahead

            @pl.when(ahead < n_groups)
            def _():
                for c in page_copies(ahead, ahead % SAMPLE_SLOTS):
                    c.start()

        seq = jnp.minimum(it // groups_per_seq, n_seqs - 1)
        in_seq = it % groups_per_seq
        q_rows = jnp.where(own, qs_ref[seq].astype(F32), 0.0).astype(BF16)
        bias = bias_tile_ref[...]

        n_new = knew_ref.shape[1]
        assert n_new == 1
        key_idx = lax.broadcasted_iota(jnp.int32, (N_HEADS, n_new), 1)
        valid = key_idx < (n_new - 1)
        s_new = jnp.sum(q_rows.astype(F32) * knew_ref[seq], axis=-1, keepdims=True)
        z_new = s_new + bias[:, :1]
        sp_new = jnp.where(valid, _softplus2(z_new), 0.0)
        a_new = jnp.where(valid, jnp.exp2(z_new - sp_new), 0.0)
        starts_seq = in_seq == 0
        rem = jnp.where(starts_seq, jnp.broadcast_to(-sp_new, rem_s_ref.shape), rem_s_ref[...])
        acc = jnp.where(starts_seq, a_new * vnew_ref[seq], acc_s_ref[...])

        pages = range(SAMPLE_PAGES_PER_GROUP)
        zs = [jnp.dot(q_rows, kbuf_ref[slot, p].astype(BF16), preferred_element_type=F32) + bias for p in pages]
        sps = [_softplus2(z) for z in zs]
        css = [jnp.dot(sp.astype(BF16), nsm_page, preferred_element_type=F32) for sp in sps]
        tots = [jnp.sum(sp, axis=-1, keepdims=True) for sp in sps]
        for p in pages:
            a = jnp.exp2(zs[p] + css[p] + rem)
            rem = rem - tots[p]
            acc = acc + lax.dot_general(a.astype(BF16), vbuf_ref[slot, p].astype(BF16), (((1,), (1,)), ((), ())),
                                        preferred_element_type=F32)
        rem_s_ref[...] = rem
        acc_s_ref[...] = acc
        it_ref[0] = it + 1

        def finish():
            @pl.when(live & (in_seq == groups_per_seq - 1))
            def _():
                os_ref[seq] = jnp.sum(jnp.where(own, acc_s_ref[...], 0.0), axis=0, keepdims=True).astype(os_ref.dtype)

        return finish

    acc_ref[...] = jnp.zeros_like(acc_ref)
    rem_ref[...] = jnp.zeros_like(rem_ref)

    def sweep(j, masked):
        finish_sample_group = sample_group()
        k_all = k_ref[j]
        v_all = v_ref[pl.ds(pl.multiple_of(j * kb, kb), kb), :]
        if masked:
            valid = (j * kb + key_in_block) < row_t
            valid_pair = jnp.concatenate([valid] * HEADS_PER_LANE_TILE, axis=1)
        zs = []
        for p in pairs:
            k2t = k_all[p * LANES:(p + 1) * LANES]
            w = jnp.concatenate([k2t * k_mask[h] for h in heads], axis=1)
            zs.append(jnp.dot(q_all[:, p * LANES:(p + 1) * LANES], w, preferred_element_type=F32) + bias_row[p])
        sps = [_softplus2(z) for z in zs]
        if masked:
            sps = [jnp.where(valid_pair, sp, 0.0) for sp in sps]
        css = []
        for sp in sps:
            sp_rows = jnp.concatenate([sp[:, h * kb:(h + 1) * kb] for h in heads], axis=0).astype(BF16)
            css.append(jnp.dot(sp_rows, nsm, preferred_element_type=F32))
        for p in pairs:
            a_parts = []
            for h in heads:
                rem = rem_ref[HEADS_PER_LANE_TILE * p + h]
                arg = (zs[p][:, h * kb:(h + 1) * kb] + css[p][h * tq:(h + 1) * tq]
                       + jnp.tile(rem, (1, kb // LANES)))
                a_h = jnp.exp2(arg)
                if masked:
                    a_h = jnp.where(valid, a_h, 0.0)
                a_parts.append(a_h.astype(BF16))
                rem_ref[HEADS_PER_LANE_TILE * p + h] = rem - jnp.sum(
                    sps[p][:, h * kb:(h + 1) * kb], axis=-1, keepdims=True)
            a = jnp.concatenate(a_parts, axis=1)
            v2 = v_all[:, p * LANES:(p + 1) * LANES]
            v_bd = jnp.concatenate([v2 * v_mask[h] for h in heads], axis=0)
            acc_ref[:, p * LANES:(p + 1) * LANES] += jnp.dot(a, v_bd, preferred_element_type=F32)
        finish_sample_group()

    blocks_per_q = tq // kb
    for d in range(blocks_per_q):
        sweep(qi * blocks_per_q + (blocks_per_q - 1 - d), True)

    n_full = qi * blocks_per_q

    def body(jj, carry):
        sweep(n_full - 1 - jj, False)
        return carry

    lax.fori_loop(0, n_full, body, 0)
    o_ref[...] = acc_ref[...].astype(o_ref.dtype)


def _sb_attention(q, kb, vb, bias2, qs3, k_new3, v_new3, k_pool, v_pool, page_table, tq, pairs_per_step):
    bsz, n_kb = kb.shape[:2]
    seq = n_kb * KEY_BLOCK
    width = pairs_per_step * LANES
    groups = D_MODEL // width
    rows_q = seq // tq
    dec, n_pages = page_table.shape
    blocks_per_q = tq // KEY_BLOCK
    n_sweeps = bsz * groups * blocks_per_q * rows_q * (rows_q + 1) // 2
    assert n_pages % SAMPLE_PAGES_PER_GROUP == 0 and n_sweeps >= dec * n_pages // SAMPLE_PAGES_PER_GROUP
    bias_tile = jnp.broadcast_to(bias2[:, None], (N_HEADS, LANES))
    whole = lambda shape: pl.BlockSpec(shape, lambda b, g, i, pt: (0,) * len(shape))
    page_buf = pltpu.VMEM((SAMPLE_SLOTS, SAMPLE_PAGES_PER_GROUP, D_MODEL, PAGE_SIZE), F32)
    grid_spec = pltpu.PrefetchScalarGridSpec(
        num_scalar_prefetch=1,
        grid=(bsz, groups, rows_q),
        in_specs=[
            pl.BlockSpec(memory_space=pltpu.SMEM),
            pl.BlockSpec((tq, width), lambda b, g, i, pt: (b * rows_q + i, g)),
            pl.BlockSpec((None, n_kb, width, KEY_BLOCK), lambda b, g, i, pt: (b, 0, g, 0)),
            pl.BlockSpec((seq, width), lambda b, g, i, pt: (b, g)),
            whole((dec, 1, D_MODEL)), whole((dec, 1, D_MODEL)), whole((dec, 1, D_MODEL)),
            whole((N_HEADS, LANES)),
            pl.BlockSpec(memory_space=pl.ANY), pl.BlockSpec(memory_space=pl.ANY),
        ],
        out_specs=[
            pl.BlockSpec((tq, width), lambda b, g, i, pt: (b * rows_q + i, g)),
            whole((dec, 1, D_MODEL)),
        ],
        scratch_shapes=[
            pltpu.VMEM((tq, width), F32),
            pltpu.VMEM((pairs_per_step * HEADS_PER_LANE_TILE, tq, LANES), F32),
            page_buf, page_buf,
            pltpu.SemaphoreType.DMA((SAMPLE_SLOTS, 2 * SAMPLE_PAGES_PER_GROUP)),
            pltpu.VMEM((N_HEADS, D_MODEL), F32),
            pltpu.VMEM((N_HEADS, PAGE_SIZE), F32),
            pltpu.SMEM((1,), jnp.int32),
        ],
    )
    return pl.pallas_call(
        functools.partial(_sb_kernel, tq=tq, pairs_per_step=pairs_per_step),
        grid_spec=grid_spec,
        out_shape=[jax.ShapeDtypeStruct((bsz * seq, D_MODEL), BF16), jax.ShapeDtypeStruct((dec, 1, D_MODEL), BF16)],
        compiler_params=_compiler_params(("arbitrary", "arbitrary", "arbitrary")),
        name="sb_attention",
    )(page_table.reshape(-1), bias2, q, kb, vb, qs3, k_new3, v_new3, bias_tile, k_pool, v_pool)


def _rglru_gates(xc, wra_ref, bra_ref, wri_ref, bri_ref, lam_ref):
    xb = xc.astype(BF16)
    r_parts = []
    i_parts = []
    for n in range(N_RNN_BLOCKS):
        xn = xb[:, n * RNN_BLOCK:(n + 1) * RNN_BLOCK]
        r_parts.append(jnp.dot(xn, wra_ref[n], preferred_element_type=F32))
        i_parts.append(jnp.dot(xn, wri_ref[n], preferred_element_type=F32))
    r = _sigmoid(jnp.concatenate(r_parts, axis=-1) + bra_ref[...])
    i = _sigmoid(jnp.concatenate(i_parts, axis=-1) + bri_ref[...])
    log_a = (-LRU_C * r) * _softplus(-lam_ref[...])
    a = jnp.exp(log_a)
    u = (xc * i) * jnp.sqrt(jnp.tanh(-log_a) * (a * a + 1.0))
    return a, u


def _rnn_scan_kernel(xr_ref, gr_ref, hist_ref, h0_ref, wc_ref, bc_ref, wra_ref, bra_ref, wri_ref, bri_ref,
                     lam_ref, out_ref, hlast_ref, xbuf_ref, a_ref, u_ref, h_ref, *, ts):
    st = pl.program_id(1)

    @pl.when(st == 0)
    def _():
        xbuf_ref[0:SUBLANES, :] = hist_ref[...]
        h_ref[...] = h0_ref[...]

    xbuf_ref[SUBLANES:SUBLANES + ts, :] = xr_ref[...]
    xc = bc_ref[...] + xbuf_ref[SUBLANES:SUBLANES + ts, :] * wc_ref[RNN_CONV_W - 1:RNN_CONV_W, :]
    for back in range(1, RNN_CONV_W):
        tap = RNN_CONV_W - 1 - back
        xc = xc + xbuf_ref[SUBLANES - back:SUBLANES - back + ts, :] * wc_ref[tap:tap + 1, :]
    xbuf_ref[0:SUBLANES, :] = xbuf_ref[ts:ts + SUBLANES, :]

    a, u = _rglru_gates(xc, wra_ref, bra_ref, wri_ref, bri_ref, lam_ref)
    a_ref[...] = a
    u_ref[...] = u

    def row(t, h):
        h = a_ref[pl.ds(t, 1), :] * h + u_ref[pl.ds(t, 1), :]
        u_ref[pl.ds(t, 1), :] = h
        return h

    h_last = lax.fori_loop(0, ts, row, h_ref[...], unroll=8)
    h_ref[...] = h_last
    hlast_ref[...] = h_last
    out_ref[...] = (jax.nn.gelu(gr_ref[...]) * u_ref[...]).astype(out_ref.dtype)


def _rnn_scan(rest, hist8, h0, wc, bc, wra, bra, wri, bri, lam, bsz, seq, ts):
    d = D_MODEL
    tiles = seq // ts
    vec = pl.BlockSpec((1, d), lambda b, s: (0, 0))
    blk = pl.BlockSpec((N_RNN_BLOCKS, RNN_BLOCK, RNN_BLOCK), lambda b, s: (0, 0, 0))
    kernel = functools.partial(_rnn_scan_kernel, ts=ts)
    return pl.pallas_call(
        kernel,
        grid=(bsz, tiles),
        in_specs=[
            pl.BlockSpec((ts, d), lambda b, s: (b * tiles + s, 0)),
            pl.BlockSpec((ts, d), lambda b, s: (b * tiles + s, 1)),
            pl.BlockSpec((None, SUBLANES, d), lambda b, s: (b, 0, 0)),
            pl.BlockSpec((None, 1, d), lambda b, s: (b, 0, 0)),
            pl.BlockSpec((RNN_CONV_W, d), lambda b, s: (0, 0)),
            vec, blk, vec, blk, vec, vec,
        ],
        out_specs=[
            pl.BlockSpec((ts, d), lambda b, s: (b * tiles + s, 0)),
            pl.BlockSpec((None, 1, d), lambda b, s: (b, 0, 0)),
        ],
        out_shape=[
            jax.ShapeDtypeStruct((bsz * seq, d), BF16),
            jax.ShapeDtypeStruct((bsz, 1, d), F32),
        ],
        scratch_shapes=[
            pltpu.VMEM((ts + SUBLANES, d), F32),
            pltpu.VMEM((ts, d), F32),
            pltpu.VMEM((ts, d), F32),
            pltpu.VMEM((1, d), F32),
        ],
        compiler_params=_compiler_params(("arbitrary", "arbitrary")),
        name="rnn_scan",
    )(rest, rest, hist8, h0, wc, bc, wra, bra, wri, bri, lam)


def _rnn_step_kernel(xr_ref, gr_ref, hist_ref, h0_ref, wc_ref, bc_ref, wra_ref, bra_ref, wri_ref, bri_ref,
                     lam_ref, out_ref, h_ref):
    xc = bc_ref[...] + xr_ref[...] * wc_ref[RNN_CONV_W - 1:RNN_CONV_W, :]
    for tap in range(RNN_CONV_W - 1):
        xc = xc + hist_ref[tap] * wc_ref[tap:tap + 1, :]
    a, u = _rglru_gates(xc, wra_ref, bra_ref, wri_ref, bri_ref, lam_ref)
    h = a * h0_ref[...] + u
    h_ref[...] = h
    out_ref[...] = (jax.nn.gelu(gr_ref[...]) * h).astype(out_ref.dtype)


def _rnn_step(rest, hist_t, h0, wc, bc, wra, bra, wri, bri, lam):
    bsz = rest.shape[0]
    d = D_MODEL
    vec = pl.BlockSpec((1, d), lambda i: (0, 0))
    blk = pl.BlockSpec((N_RNN_BLOCKS, RNN_BLOCK, RNN_BLOCK), lambda i: (0, 0, 0))
    return pl.pallas_call(
        _rnn_step_kernel,
        grid=(1,),
        in_specs=[
            pl.BlockSpec((bsz, d), lambda i: (0, 0)),
            pl.BlockSpec((bsz, d), lambda i: (0, 1)),
            pl.BlockSpec((RNN_CONV_W - 1, bsz, d), lambda i: (0, 0, 0)),
            pl.BlockSpec((bsz, d), lambda i: (0, 0)),
            pl.BlockSpec((RNN_CONV_W, d), lambda i: (0, 0)),
            vec, blk, vec, blk, vec, vec,
        ],
        out_specs=[pl.BlockSpec((bsz, d), lambda i: (0, 0)), pl.BlockSpec((bsz, d), lambda i: (0, 0))],
        out_shape=[jax.ShapeDtypeStruct((bsz, d), BF16), jax.ShapeDtypeStruct((bsz, d), F32)],
        compiler_params=_compiler_params(("arbitrary",)),
        name="rnn_step",
    )(rest, rest, hist_t, h0, wc, bc, wra, bra, wri, bri, lam)


def _merge_kernel(x_ref, att_ref, rnn_ref, ga_ref, gb_ref, wpa_ref, wpb_ref, wo_ref, g_ref, b_ref, o_ref):
    pa = jnp.dot(att_ref[...], wpa_ref[...], preferred_element_type=F32)
    pb = jnp.dot(rnn_ref[...], wpb_ref[...], preferred_element_type=F32)
    merged = _sigmoid(ga_ref[...]) * pa + _sigmoid(gb_ref[...]) * pb
    y = jnp.dot(merged.astype(BF16), wo_ref[...], preferred_element_type=F32)
    o_ref[...] = _layer_norm(DEEPNORM_ALPHA * x_ref[...] + y, g_ref[...], b_ref[...])


def _merge(x, att, rnn, rest, wpa, wpb, wo, g, b, tm):
    t = x.shape[0]
    d = D_MODEL
    row = pl.BlockSpec((tm, d), lambda i: (i, 0))
    mat = pl.BlockSpec((d, d), lambda i: (0, 0))
    vec = pl.BlockSpec((1, d), lambda i: (0, 0))
    return pl.pallas_call(
        _merge_kernel,
        grid=(t // tm,),
        in_specs=[row, row, row,
                  pl.BlockSpec((tm, d), lambda i: (i, 2)), pl.BlockSpec((tm, d), lambda i: (i, 3)),
                  mat, mat, mat, vec, vec],
        out_specs=row,
        out_shape=jax.ShapeDtypeStruct((t, d), F32),
        compiler_params=_compiler_params(("arbitrary",)),
        name="merge",
    )(x, att, rnn, rest, rest, wpa, wpb, wo, g, b)


def _ffn_kernel(*refs, tm, tf, rows_per_seq, per_row_history):
    if per_row_history:
        (x1_ref, wu_ref, wg_ref, wd_ref, wf_ref, bf_ref, g2_ref, b2_ref, pe_ref, wple_ref, wpleg_ref, bpleg_ref,
         g3_ref, b3_ref, hist_ref, o_ref, gate_ref, acc_ref, xb_ref, up_ref, hid_ref, gbuf_ref) = refs
    else:
        (x1_ref, wu_ref, wg_ref, wd_ref, wf_ref, bf_ref, g2_ref, b2_ref, pe_ref, wple_ref, wpleg_ref, bpleg_ref,
         g3_ref, b3_ref, o_ref, gate_ref, acc_ref, xb_ref, up_ref, hid_ref, gbuf_ref, gtail_ref) = refs
    i = pl.program_id(0)
    j = pl.program_id(1)

    @pl.when(j == 0)
    def _():
        xb_ref[...] = x1_ref[...].astype(BF16)
        acc_ref[...] = jnp.zeros_like(acc_ref)

    if not per_row_history:
        first_of_seq = (i % (rows_per_seq // tm)) == 0

        @pl.when(first_of_seq)
        def _():
            gbuf_ref[0:SUBLANES, :] = jnp.zeros((SUBLANES, tf), F32)

        @pl.when(jnp.logical_not(first_of_seq))
        def _():
            gbuf_ref[0:SUBLANES, :] = gtail_ref[j]

    n_parts = max(1, tm // FFN_PART_ROWS)
    part = tm // n_parts
    rc = min(part, FFN_ROW_CHUNK)
    goff = 0 if per_row_history else SUBLANES

    def project(r0):
        xb = xb_ref[r0:r0 + part, :]
        up_ref[r0:r0 + part, :] = jnp.dot(xb, wu_ref[...], preferred_element_type=F32)
        gbuf_ref[goff + r0:goff + r0 + part, :] = jnp.dot(xb, wg_ref[...], preferred_element_type=F32)

    def gate_chain(r0):
        for c0 in range(r0, r0 + part, rc):
            if per_row_history:
                gate_c = bf_ref[...] + gbuf_ref[c0:c0 + rc, :] * wf_ref[FFN_CONV_W - 1:FFN_CONV_W, :]
                for tap in range(FFN_CONV_W - 1):
                    gate_c = gate_c + hist_ref[tap, c0:c0 + rc, :] * wf_ref[tap:tap + 1, :]
            else:
                window = gbuf_ref[c0:c0 + rc + SUBLANES, :]
                gate_c = bf_ref[...]
                for back in range(FFN_CONV_W):
                    tap = FFN_CONV_W - 1 - back
                    gate_c = gate_c + window[SUBLANES - back:SUBLANES - back + rc] * wf_ref[tap:tap + 1, :]
            hid_ref[c0:c0 + rc, :] = (jax.nn.gelu(gate_c) * up_ref[c0:c0 + rc, :]).astype(BF16)

    def down(r0):
        acc_ref[r0:r0 + part, :] += jnp.dot(hid_ref[r0:r0 + part, :], wd_ref[...], preferred_element_type=F32)

    project(0)
    for k in range(n_parts):
        if k + 1 < n_parts:
            project((k + 1) * part)
        gate_chain(k * part)
        down(k * part)

    if per_row_history:
        gate_ref[...] = gbuf_ref[...]
    else:
        tail = gbuf_ref[tm:tm + SUBLANES, :]
        gtail_ref[j] = tail
        gate_ref[...] = tail

    @pl.when(j == pl.num_programs(1) - 1)
    def _():
        x2 = _layer_norm(DEEPNORM_ALPHA * x1_ref[...] + acc_ref[...], g2_ref[...], b2_ref[...])
        emb = jnp.dot(pe_ref[...].astype(BF16), wple_ref[...], preferred_element_type=F32)
        gate_e = _sigmoid(jnp.dot(x2.astype(BF16), wpleg_ref[...], preferred_element_type=F32) + bpleg_ref[...])
        o_ref[...] = _layer_norm(DEEPNORM_ALPHA * x2 + emb * gate_e, g3_ref[...], b3_ref[...])


def _ffn(x1, pe, wup, wdown, wf, bfc, g2, b2, wple, wpleg, bpleg, g3, b3, tm, tf, rows_per_seq, hist_t=None):
    t = x1.shape[0]
    d = D_MODEL
    n_chunks = D_FF // tf
    per_row_history = hist_t is not None
    row = pl.BlockSpec((tm, d), lambda i, j: (i, 0))
    vec = pl.BlockSpec((1, d), lambda i, j: (0, 0))
    fvec = pl.BlockSpec((1, tf), lambda i, j: (0, j))
    in_specs = [
        row,
        pl.BlockSpec((d, tf), lambda i, j: (0, j)),
        pl.BlockSpec((d, tf), lambda i, j: (0, n_chunks + j)),
        pl.BlockSpec((tf, d), lambda i, j: (j, 0)),
        pl.BlockSpec((FFN_CONV_W, tf), lambda i, j: (0, j)),
        fvec, vec, vec,
        pl.BlockSpec((tm, D_PLE), lambda i, j: (i, 0)),
        pl.BlockSpec((D_PLE, d), lambda i, j: (0, 0)),
        pl.BlockSpec((d, d), lambda i, j: (0, 0)),
        vec, vec, vec,
    ]
    args = [x1, wup, wup, wdown, wf, bfc, g2, b2, pe, wple, wpleg, bpleg, g3, b3]
    scratch = [pltpu.VMEM((tm, d), F32), pltpu.VMEM((tm, d), BF16), pltpu.VMEM((tm, tf), F32), pltpu.VMEM((tm, tf), BF16)]
    if per_row_history:
        scratch.append(pltpu.VMEM((tm, tf), F32))
        in_specs.append(pl.BlockSpec((FFN_CONV_W - 1, tm, tf), lambda i, j: (0, i, j)))
        args.append(hist_t)
        gate_spec = pl.BlockSpec((tm, tf), lambda i, j: (i, j))
        gate_shape = jax.ShapeDtypeStruct((t, D_FF), F32)
    else:
        gate_spec = pl.BlockSpec((None, SUBLANES, tf), lambda i, j: (i, 0, j))
        gate_shape = jax.ShapeDtypeStruct((t // tm, SUBLANES, D_FF), F32)
        scratch += [pltpu.VMEM((tm + SUBLANES, tf), F32), pltpu.VMEM((n_chunks, SUBLANES, tf), F32)]
    kernel = functools.partial(_ffn_kernel, tm=tm, tf=tf, rows_per_seq=rows_per_seq, per_row_history=per_row_history)
    return pl.pallas_call(
        kernel,
        grid=(t // tm, n_chunks),
        in_specs=in_specs,
        out_specs=[row, gate_spec],
        out_shape=[jax.ShapeDtypeStruct((t, d), F32), gate_shape],
        scratch_shapes=scratch,
        compiler_params=_compiler_params(("arbitrary", "arbitrary")),
        name="ffn",
    )(*args)


def kernel(x_prompt, x_sample, p_prompt, p_sample, cache_k, cache_v, page_table, state_rnn_conv, state_rnn_h, state_ffn_conv, w_in, sb_bias, w_rconv, b_rconv, w_ra, b_ra, w_ri, b_ri, lru_lambda, w_pa, w_pb, w_o, ln1_g, ln1_b, w_up, w_fconv, b_fconv, w_down, ln2_g, ln2_b, w_ple, w_pleg, b_pleg, ln3_g, ln3_b):
    assert w_in.shape[0] == DEPTH == 1
    bsz, seq, d = x_prompt.shape
    dec = x_sample.shape[0]
    assert x_sample.shape[1] == 1 and d == D_MODEL
    n_phys = cache_k.shape[1]

    row_vec = lambda a: a.reshape(1, -1)
    w_in_bf = w_in[0].astype(BF16)
    wra, wri = w_ra[0].astype(BF16), w_ri[0].astype(BF16)
    wpa, wpb, wo = w_pa[0].astype(BF16), w_pb[0].astype(BF16), w_o[0].astype(BF16)
    wup, wdown = w_up[0].astype(BF16), w_down[0].astype(BF16)
    wple, wpleg = w_ple[0].astype(BF16), w_pleg[0].astype(BF16)
    rnn_w = (w_rconv[0], row_vec(b_rconv[0]), wra, row_vec(b_ra[0]), wri, row_vec(b_ri[0]), row_vec(lru_lambda[0]))
    ffn_w = (wup, wdown, w_fconv[0], row_vec(b_fconv[0]), row_vec(ln2_g[0]), row_vec(ln2_b[0]),
             wple, wpleg, row_vec(b_pleg[0]), row_vec(ln3_g[0]), row_vec(ln3_b[0]))
    merge_w = (wpa, wpb, wo, row_vec(ln1_g[0]), row_vec(ln1_b[0]))

    xp = x_prompt.reshape(bsz * seq, d)
    xs = x_sample.reshape(dec, d)
    wt_kv_bf = jnp.transpose(w_in[0][:, d:3 * d]).astype(BF16)
    q_p, kt_p, vt_p, kb_p, vb_p, rest_p = _inproj(xp, w_in_bf, tm=512, wt_kv_bf=wt_kv_bf, rows_per_seq=seq)
    q_s, k_s, v_s, rest_s = _inproj(xs, w_in_bf, tm=dec)

    bias2 = sb_bias[0] * LOG2E
    k_pool = jnp.transpose(cache_k[0], (0, 2, 3, 1)).reshape(n_phys, d, PAGE_SIZE)
    v_pool = jnp.transpose(cache_v[0], (0, 2, 3, 1)).reshape(n_phys, d, PAGE_SIZE)
    att_p, att_s = _sb_attention(q_p, kb_p, vb_p, bias2, q_s.reshape(dec, 1, d), k_s.reshape(dec, 1, d),
                                 v_s.reshape(dec, 1, d), k_pool, v_pool, page_table, tq=256, pairs_per_step=4)
    att_s = att_s.reshape(dec, d)

    rnn_p, h_p = _rnn_scan(rest_p, jnp.zeros((bsz, SUBLANES, d), F32), jnp.zeros((bsz, 1, d), F32), *rnn_w,
                           bsz=bsz, seq=seq, ts=256)
    x1_p = _merge(xp, att_p, rnn_p, rest_p, *merge_w, tm=512)
    ffn_tm = 512
    y_p, gate_tails = _ffn(x1_p, p_prompt[0].reshape(bsz * seq, D_PLE), *ffn_w, tm=ffn_tm, tf=512, rows_per_seq=seq)
    rest_p4 = rest_p.reshape(bsz, seq, 4 * d)
    rconv_p = rest_p4[:, seq - (RNN_CONV_W - 1):, :d]
    fconv_p = gate_tails.reshape(bsz, seq // ffn_tm, SUBLANES, D_FF)[:, -1, SUBLANES - (FFN_CONV_W - 1):, :]
    hd = (N_HEADS, HEAD_DIM)
    k_p = jnp.transpose(kt_p.reshape(bsz, *hd, seq), (0, 3, 1, 2))
    v_p = jnp.transpose(vt_p.reshape(bsz, *hd, seq), (0, 3, 1, 2))

    rnn_s, h_s = _rnn_step(rest_s, jnp.swapaxes(state_rnn_conv[0], 0, 1), state_rnn_h[0], *rnn_w)
    x1_s = _merge(xs, att_s, rnn_s, rest_s, *merge_w, tm=dec)
    y_s, gate_s = _ffn(x1_s, p_sample[0].reshape(dec, D_PLE), *ffn_w, tm=dec, tf=512, rows_per_seq=1,
                       hist_t=jnp.swapaxes(state_ffn_conv[0], 0, 1))
    rconv_s = jnp.concatenate([state_rnn_conv[0][:, 1:], rest_s[:, None, :d]], axis=1)
    fconv_s = jnp.concatenate([state_ffn_conv[0][:, 1:], gate_s[:, None, :]], axis=1)

    return (y_p.reshape(bsz, seq, d), y_s.reshape(dec, 1, d),
            k_p[None], v_p[None],
            rconv_p[None], h_p.reshape(1, bsz, d), fconv_p[None],
            k_s.reshape(1, dec, 1, *hd), v_s.reshape(1, dec, 1, *hd),
            rconv_s[None], h_s[None], fconv_s[None])
```

```python
import functools

import jax
import jax.numpy as jnp
from jax import lax
from jax.experimental import pallas as pl
from jax.experimental.pallas import tpu as pltpu

F32 = jnp.float32
BF16 = jnp.bfloat16

D_MODEL = 1024
N_HEADS = 16
HEAD_DIM = 64
N_RNN_BLOCKS = 8
RNN_BLOCK = D_MODEL // N_RNN_BLOCKS
RNN_CONV_W = 4
LRU_C = 8.0
D_FF = 3 * D_MODEL
FFN_CONV_W = 3
D_PLE = 256
PAGE_SIZE = 128
LN_EPS = 1e-5
DEPTH = 1
DEEPNORM_ALPHA = (2.0 * DEPTH) ** 0.25
N_IN_SPLITS = 7

LANES = 128
SUBLANES = 8
HEADS_PER_LANE_TILE = LANES // HEAD_DIM
FFN_ROW_CHUNK = 64
FFN_PART_ROWS = 256
KEY_BLOCK = 256
LOG2E = 1.4426950408889634
VMEM_LIMIT_BYTES = 56 * 1024 * 1024


def _compiler_params(semantics):
    return pltpu.CompilerParams(dimension_semantics=semantics, vmem_limit_bytes=VMEM_LIMIT_BYTES)


def _layer_norm(x, g, b):
    mu = jnp.mean(x, axis=-1, keepdims=True)
    xc = x - mu
    var = jnp.mean(xc * xc, axis=-1, keepdims=True)
    return xc * lax.rsqrt(var + LN_EPS) * g + b


def _sigmoid(x):
    return 0.5 * jnp.tanh(0.5 * x) + 0.5


def _softplus(z):
    return jnp.maximum(z, 0.0) + jnp.log(1.0 + jnp.exp(-jnp.abs(z)))


def _softplus2(z2):
    return jnp.maximum(z2, 0.0) + jnp.log2(1.0 + jnp.exp2(-jnp.abs(z2)))


def _neg_suffix_sum_matrix(n):
    j = lax.broadcasted_iota(jnp.int32, (n, n), 0)
    s = lax.broadcasted_iota(jnp.int32, (n, n), 1)
    return jnp.where(j >= s, -1.0, 0.0).astype(BF16)


def _inproj_kernel(*refs, tm, key_major):
    if key_major:
        x_ref, w_ref, wt_ref, q_ref, kf_ref, vf_ref, kb_ref, vb_ref, rest_ref = refs
    else:
        x_ref, w_ref, q_ref, kf_ref, vf_ref, rest_ref = refs
    j = pl.program_id(0)

    def project():
        return jnp.dot(x_ref[...], w_ref[...], preferred_element_type=F32)

    def project_key_major():
        return lax.dot_general(wt_ref[...], x_ref[...], (((1,), (1,)), ((), ())), preferred_element_type=F32)

    @pl.when(j == 0)
    def _():
        q_ref[...] = (project() * (LOG2E * HEAD_DIM ** -0.5)).astype(BF16)

    @pl.when(j == 1)
    def _():
        if key_major:
            kt = project_key_major()
            kf_ref[...] = kt
            for kb in range(tm // KEY_BLOCK):
                kb_ref[kb] = kt[:, kb * KEY_BLOCK:(kb + 1) * KEY_BLOCK].astype(BF16)
        else:
            kf_ref[...] = project()

    @pl.when(j == 2)
    def _():
        if key_major:
            vf_ref[...] = project_key_major()
            vb_ref[...] = project().astype(BF16)
        else:
            vf_ref[...] = project()

    @pl.when(j >= 3)
    def _():
        rest_ref[...] = project()


def _inproj(x_bf, w_in_bf, tm, wt_kv_bf=None, rows_per_seq=None):
    t = x_bf.shape[0]
    d = D_MODEL
    n_tiles = t // tm
    key_major = wt_kv_bf is not None

    def parked(j, i, first_split, last_split):
        return jnp.where(j < first_split, 0, jnp.where(j > last_split, n_tiles - 1, i))

    x_spec = pl.BlockSpec((tm, d), lambda j, i: (i, 0))
    q_spec = pl.BlockSpec((tm, d), lambda j, i: (parked(j, i, 0, 0), 0))
    rest_spec = pl.BlockSpec((tm, d), lambda j, i: (parked(j, i, 3, N_IN_SPLITS - 1), jnp.maximum(j - 3, 0)))
    q_shape = jax.ShapeDtypeStruct((t, d), BF16)
    rest_shape = jax.ShapeDtypeStruct((t, 4 * d), F32)
    if key_major:
        tiles = rows_per_seq // tm
        bsz = t // rows_per_seq
        kb_per_tile = tm // KEY_BLOCK
        in_specs = [
            x_spec,
            pl.BlockSpec((d, d), lambda j, i: (0, jnp.where(j == 1, 0, j))),
            pl.BlockSpec((d, d), lambda j, i: (jnp.clip(j - 1, 0, 1), 0)),
        ]

        def f_spec(split):
            def index(j, i):
                p = parked(j, i, split, split)
                return (p // tiles, 0, p % tiles)
            return pl.BlockSpec((None, d, tm), index)

        def kb_index(j, i):
            p = parked(j, i, 1, 1)
            return (p // tiles, p % tiles, 0, 0)

        kb_spec = pl.BlockSpec((None, kb_per_tile, d, KEY_BLOCK), kb_index)
        vb_spec = pl.BlockSpec((tm, d), lambda j, i: (parked(j, i, 2, 2), 0))
        f_shape = jax.ShapeDtypeStruct((bsz, d, rows_per_seq), F32)
        kb_shape = jax.ShapeDtypeStruct((bsz, rows_per_seq // KEY_BLOCK, d, KEY_BLOCK), BF16)
        out_specs = [q_spec, f_spec(1), f_spec(2), kb_spec, vb_spec, rest_spec]
        out_shape = [q_shape, f_shape, f_shape, kb_shape, q_shape, rest_shape]
        args = (x_bf, w_in_bf, wt_kv_bf)
    else:
        in_specs = [x_spec, pl.BlockSpec((d, d), lambda j, i: (0, j))]
        f_shape = jax.ShapeDtypeStruct((t, d), F32)
        out_specs = [q_spec,
                     pl.BlockSpec((tm, d), lambda j, i: (parked(j, i, 1, 1), 0)),
                     pl.BlockSpec((tm, d), lambda j, i: (parked(j, i, 2, 2), 0)),
                     rest_spec]
        out_shape = [q_shape, f_shape, f_shape, rest_shape]
        args = (x_bf, w_in_bf)
    return pl.pallas_call(
        functools.partial(_inproj_kernel, tm=tm, key_major=key_major),
        grid=(N_IN_SPLITS, n_tiles),
        in_specs=in_specs,
        out_specs=out_specs,
        out_shape=out_shape,
        compiler_params=_compiler_params(("arbitrary", "arbitrary")),
        name="inproj",
    )(*args)


SAMPLE_PAGES_PER_GROUP = 4
SAMPLE_SLOTS = 3


def _sb_kernel(pt_ref, bias_ref, q_ref, k_ref, v_ref, qs_ref, knew_ref, vnew_ref, bias_tile_ref, kpool_ref, vpool_ref,
               o_ref, os_ref, acc_ref, rem_ref, kbuf_ref, vbuf_ref, sem_ref, acc_s_ref, rem_s_ref, it_ref,
               *, tq, pairs_per_step):
    group = pl.program_id(1)
    qi = pl.program_id(2)
    heads = range(HEADS_PER_LANE_TILE)
    pairs = range(pairs_per_step)
    kb = KEY_BLOCK
    col_head = lax.broadcasted_iota(jnp.int32, (1, HEADS_PER_LANE_TILE * kb), 1) // kb
    first_head = [HEADS_PER_LANE_TILE * (group * pairs_per_step + p) for p in pairs]
    bias_row = [jnp.where(col_head == 0, bias_ref[first_head[p]], bias_ref[first_head[p] + 1]) for p in pairs]
    k_row_head = lax.broadcasted_iota(jnp.int32, (LANES, kb), 0) // HEAD_DIM
    v_lane_head = lax.broadcasted_iota(jnp.int32, (kb, LANES), 1) // HEAD_DIM
    k_mask = [jnp.where(k_row_head == h, 1.0, 0.0).astype(BF16) for h in heads]
    v_mask = [jnp.where(v_lane_head == h, 1.0, 0.0).astype(BF16) for h in heads]
    nsm = _neg_suffix_sum_matrix(kb)
    q_all = q_ref[...]
    key_in_block = lax.broadcasted_iota(jnp.int32, (1, kb), 1)
    row_t = qi * tq + lax.broadcasted_iota(jnp.int32, (tq, 1), 0)

    n_seqs = qs_ref.shape[0]
    n_pages = pt_ref.shape[0] // n_seqs
    groups_per_seq = n_pages // SAMPLE_PAGES_PER_GROUP
    n_groups = n_seqs * groups_per_seq
    lookahead = SAMPLE_SLOTS - 1
    own = (lax.broadcasted_iota(jnp.int32, (N_HEADS, D_MODEL), 0)
           == lax.broadcasted_iota(jnp.int32, (N_HEADS, D_MODEL), 1) // HEAD_DIM)
    nsm_page = _neg_suffix_sum_matrix(PAGE_SIZE)

    def page_copies(g, slot):
        seq = g // groups_per_seq
        latest = (n_pages - 1) - (g % groups_per_seq) * SAMPLE_PAGES_PER_GROUP
        copies = []
        for p in range(SAMPLE_PAGES_PER_GROUP):
            phys = pt_ref[seq * n_pages + latest - p]
            copies.append(pltpu.make_async_copy(kpool_ref.at[phys], kbuf_ref.at[slot, p], sem_ref.at[slot, 2 * p]))
            copies.append(pltpu.make_async_copy(vpool_ref.at[phys], vbuf_ref.at[slot, p], sem_ref.at[slot, 2 * p + 1]))
        return copies

    first_step = (pl.program_id(0) == 0) & (group == 0) & (qi == 0)

    @pl.when(first_step)
    def _():
        it_ref[0] = 0
        acc_s_ref[...] = jnp.zeros_like(acc_s_ref)
        rem_s_ref[...] = jnp.zeros_like(rem_s_ref)
        for g in range(lookahead):
            for c in page_copies(g, g):
                c.start()

    def sample_group():
        it = it_ref[0]
        slot = it % SAMPLE_SLOTS
        live = it < n_groups

        @pl.when(live)
        def _():
            for c in page_copies(it, slot):
                c.wait()
            ahead = it + lookahead

            @pl.when(ahead < n_groups)
            def _():
                for c in page_copies(ahead, ahead % SAMPLE_SLOTS):
                    c.start()

        seq = jnp.minimum(it // groups_per_seq, n_seqs - 1)
        in_seq = it % groups_per_seq
        q_rows = jnp.where(own, qs_ref[seq].astype(F32), 0.0).astype(BF16)
        bias = bias_tile_ref[...]

        n_new = knew_ref.shape[1]
        assert n_new == 1
        key_idx = lax.broadcasted_iota(jnp.int32, (N_HEADS, n_new), 1)
        valid = key_idx < (n_new - 1)
        s_new = jnp.sum(q_rows.astype(F32) * knew_ref[seq], axis=-1, keepdims=True)
        z_new = s_new + bias[:, :1]
        sp_new = jnp.where(valid, _softplus2(z_new), 0.0)
        a_new = jnp.where(valid, jnp.exp2(z_new - sp_new), 0.0)
        starts_seq = in_seq == 0
        rem = jnp.where(starts_seq, jnp.broadcast_to(-sp_new, rem_s_ref.shape), rem_s_ref[...])
        acc = jnp.where(starts_seq, a_new * vnew_ref[seq], acc_s_ref[...])

        pages = range(SAMPLE_PAGES_PER_GROUP)
        zs = [jnp.dot(q_rows, kbuf_ref[slot, p].astype(BF16), preferred_element_type=F32) + bias for p in pages]
        sps = [_softplus2(z) for z in zs]
        css = [jnp.dot(sp.astype(BF16), nsm_page, preferred_element_type=F32) for sp in sps]
        tots = [jnp.sum(sp, axis=-1, keepdims=True) for sp in sps]
        for p in pages:
            a = jnp.exp2(zs[p] + css[p] + rem)
            rem = rem - tots[p]
            acc = acc + lax.dot_general(a.astype(BF16), vbuf_ref[slot, p].astype(BF16), (((1,), (1,)), ((), ())),
                                        preferred_element_type=F32)
        rem_s_ref[...] = rem
        acc_s_ref[...] = acc
        it_ref[0] = it + 1

        def finish():
            @pl.when(live & (in_seq == groups_per_seq - 1))
            def _():
                os_ref[seq] = jnp.sum(jnp.where(own, acc_s_ref[...], 0.0), axis=0, keepdims=True).astype(os_ref.dtype)

        return finish

    acc_ref[...] = jnp.zeros_like(acc_ref)
    rem_ref[...] = jnp.zeros_like(rem_ref)

    def sweep(j, masked):
        finish_sample_group = sample_group()
        k_all = k_ref[j]
        v_all = v_ref[pl.ds(pl.multiple_of(j * kb, kb), kb), :]
        if masked:
            valid = (j * kb + key_in_block) < row_t
            valid_pair = jnp.concatenate([valid] * HEADS_PER_LANE_TILE, axis=1)
        zs = []
        for p in pairs:
            k2t = k_all[p * LANES:(p + 1) * LANES]
            w = jnp.concatenate([k2t * k_mask[h] for h in heads], axis=1)
            zs.append(jnp.dot(q_all[:, p * LANES:(p + 1) * LANES], w, preferred_element_type=F32) + bias_row[p])
        sps = [_softplus2(z) for z in zs]
        if masked:
            sps = [jnp.where(valid_pair, sp, 0.0) for sp in sps]
        css = []
        for sp in sps:
            sp_rows = jnp.concatenate([sp[:, h * kb:(h + 1) * kb] for h in heads], axis=0).astype(BF16)
            css.append(jnp.dot(sp_rows, nsm, preferred_element_type=F32))
        for p in pairs:
            a_parts = []
            for h in heads:
                rem = rem_ref[HEADS_PER_LANE_TILE * p + h]
                arg = (zs[p][:, h * kb:(h + 1) * kb] + css[p][h * tq:(h + 1) * tq]
                       + jnp.tile(rem, (1, kb // LANES)))
                a_h = jnp.exp2(arg)
                if masked:
                    a_h = jnp.where(valid, a_h, 0.0)
                a_parts.append(a_h.astype(BF16))
                rem_ref[HEADS_PER_LANE_TILE * p + h] = rem - jnp.sum(
                    sps[p][:, h * kb:(h + 1) * kb], axis=-1, keepdims=True)
            a = jnp.concatenate(a_parts, axis=1)
            v2 = v_all[:, p * LANES:(p + 1) * LANES]
            v_bd = jnp.concatenate([v2 * v_mask[h] for h in heads], axis=0)
            acc_ref[:, p * LANES:(p + 1) * LANES] += jnp.dot(a, v_bd, preferred_element_type=F32)
        finish_sample_group()

    blocks_per_q = tq // kb
    for d in range(blocks_per_q):
        sweep(qi * blocks_per_q + (blocks_per_q - 1 - d), True)

    n_full = qi * blocks_per_q

    def body(jj, carry):
        sweep(n_full - 1 - jj, False)
        return carry

    lax.fori_loop(0, n_full, body, 0)
    o_ref[...] = acc_ref[...].astype(o_ref.dtype)


def _sb_attention(q, kb, vb, bias2, qs3, k_new3, v_new3, k_pool, v_pool, page_table, tq, pairs_per_step):
    bsz, n_kb = kb.shape[:2]
    seq = n_kb * KEY_BLOCK
    width = pairs_per_step * LANES
    groups = D_MODEL // width
    rows_q = seq // tq
    dec, n_pages = page_table.shape
    blocks_per_q = tq // KEY_BLOCK
    n_sweeps = bsz * groups * blocks_per_q * rows_q * (rows_q + 1) // 2
    assert n_pages % SAMPLE_PAGES_PER_GROUP == 0 and n_sweeps >= dec * n_pages // SAMPLE_PAGES_PER_GROUP
    bias_tile = jnp.broadcast_to(bias2[:, None], (N_HEADS, LANES))
    whole = lambda shape: pl.BlockSpec(shape, lambda b, g, i, pt: (0,) * len(shape))
    page_buf = pltpu.VMEM((SAMPLE_SLOTS, SAMPLE_PAGES_PER_GROUP, D_MODEL, PAGE_SIZE), F32)
    grid_spec = pltpu.PrefetchScalarGridSpec(
        num_scalar_prefetch=1,
        grid=(bsz, groups, rows_q),
        in_specs=[
            pl.BlockSpec(memory_space=pltpu.SMEM),
            pl.BlockSpec((tq, width), lambda b, g, i, pt: (b * rows_q + i, g)),
            pl.BlockSpec((None, n_kb, width, KEY_BLOCK), lambda b, g, i, pt: (b, 0, g, 0)),
            pl.BlockSpec((seq, width), lambda b, g, i, pt: (b, g)),
            whole((dec, 1, D_MODEL)), whole((dec, 1, D_MODEL)), whole((dec, 1, D_MODEL)),
            whole((N_HEADS, LANES)),
            pl.BlockSpec(memory_space=pl.ANY), pl.BlockSpec(memory_space=pl.ANY),
        ],
        out_specs=[
            pl.BlockSpec((tq, width), lambda b, g, i, pt: (b * rows_q + i, g)),
            whole((dec, 1, D_MODEL)),
        ],
        scratch_shapes=[
            pltpu.VMEM((tq, width), F32),
            pltpu.VMEM((pairs_per_step * HEADS_PER_LANE_TILE, tq, LANES), F32),
            page_buf, page_buf,
            pltpu.SemaphoreType.DMA((SAMPLE_SLOTS, 2 * SAMPLE_PAGES_PER_GROUP)),
            pltpu.VMEM((N_HEADS, D_MODEL), F32),
            pltpu.VMEM((N_HEADS, PAGE_SIZE), F32),
            pltpu.SMEM((1,), jnp.int32),
        ],
    )
    return pl.pallas_call(
        functools.partial(_sb_kernel, tq=tq, pairs_per_step=pairs_per_step),
        grid_spec=grid_spec,
        out_shape=[jax.ShapeDtypeStruct((bsz * seq, D_MODEL), BF16), jax.ShapeDtypeStruct((dec, 1, D_MODEL), BF16)],
        compiler_params=_compiler_params(("arbitrary", "arbitrary", "arbitrary")),
        name="sb_attention",
    )(page_table.reshape(-1), bias2, q, kb, vb, qs3, k_new3, v_new3, bias_tile, k_pool, v_pool)


def _rglru_gates(xc, wra_ref, bra_ref, wri_ref, bri_ref, lam_ref):
    xb = xc.astype(BF16)
    r_parts = []
    i_parts = []
    for n in range(N_RNN_BLOCKS):
        xn = xb[:, n * RNN_BLOCK:(n + 1) * RNN_BLOCK]
        r_parts.append(jnp.dot(xn, wra_ref[n], preferred_element_type=F32))
        i_parts.append(jnp.dot(xn, wri_ref[n], preferred_element_type=F32))
    r = _sigmoid(jnp.concatenate(r_parts, axis=-1) + bra_ref[...])
    i = _sigmoid(jnp.concatenate(i_parts, axis=-1) + bri_ref[...])
    log_a = (-LRU_C * r) * _softplus(-lam_ref[...])
    a = jnp.exp(log_a)
    u = (xc * i) * jnp.sqrt(jnp.tanh(-log_a) * (a * a + 1.0))
    return a, u


def _rnn_scan_kernel(xr_ref, gr_ref, hist_ref, h0_ref, wc_ref, bc_ref, wra_ref, bra_ref, wri_ref, bri_ref,
                     lam_ref, out_ref, hlast_ref, xbuf_ref, a_ref, u_ref, h_ref, *, ts):
    st = pl.program_id(1)

    @pl.when(st == 0)
    def _():
        xbuf_ref[0:SUBLANES, :] = hist_ref[...]
        h_ref[...] = h0_ref[...]

    xbuf_ref[SUBLANES:SUBLANES + ts, :] = xr_ref[...]
    xc = bc_ref[...] + xbuf_ref[SUBLANES:SUBLANES + ts, :] * wc_ref[RNN_CONV_W - 1:RNN_CONV_W, :]
    for back in range(1, RNN_CONV_W):
        tap = RNN_CONV_W - 1 - back
        xc = xc + xbuf_ref[SUBLANES - back:SUBLANES - back + ts, :] * wc_ref[tap:tap + 1, :]
    xbuf_ref[0:SUBLANES, :] = xbuf_ref[ts:ts + SUBLANES, :]

    a, u = _rglru_gates(xc, wra_ref, bra_ref, wri_ref, bri_ref, lam_ref)
    a_ref[...] = a
    u_ref[...] = u

    def row(t, h):
        h = a_ref[pl.ds(t, 1), :] * h + u_ref[pl.ds(t, 1), :]
        u_ref[pl.ds(t, 1), :] = h
        return h

    h_last = lax.fori_loop(0, ts, row, h_ref[...], unroll=8)
    h_ref[...] = h_last
    hlast_ref[...] = h_last
    out_ref[...] = (jax.nn.gelu(gr_ref[...]) * u_ref[...]).astype(out_ref.dtype)


def _rnn_scan(rest, hist8, h0, wc, bc, wra, bra, wri, bri, lam, bsz, seq, ts):
    d = D_MODEL
    tiles = seq // ts
    vec = pl.BlockSpec((1, d), lambda b, s: (0, 0))
    blk = pl.BlockSpec((N_RNN_BLOCKS, RNN_BLOCK, RNN_BLOCK), lambda b, s: (0, 0, 0))
    kernel = functools.partial(_rnn_scan_kernel, ts=ts)
    return pl.pallas_call(
        kernel,
        grid=(bsz, tiles),
        in_specs=[
            pl.BlockSpec((ts, d), lambda b, s: (b * tiles + s, 0)),
            pl.BlockSpec((ts, d), lambda b, s: (b * tiles + s, 1)),
            pl.BlockSpec((None, SUBLANES, d), lambda b, s: (b, 0, 0)),
            pl.BlockSpec((None, 1, d), lambda b, s: (b, 0, 0)),
            pl.BlockSpec((RNN_CONV_W, d), lambda b, s: (0, 0)),
            vec, blk, vec, blk, vec, vec,
        ],
        out_specs=[
            pl.BlockSpec((ts, d), lambda b, s: (b * tiles + s, 0)),
            pl.BlockSpec((None, 1, d), lambda b, s: (b, 0, 0)),
        ],
        out_shape=[
            jax.ShapeDtypeStruct((bsz * seq, d), BF16),
            jax.ShapeDtypeStruct((bsz, 1, d), F32),
        ],
        scratch_shapes=[
            pltpu.VMEM((ts + SUBLANES, d), F32),
            pltpu.VMEM((ts, d), F32),
            pltpu.VMEM((ts, d), F32),
            pltpu.VMEM((1, d), F32),
        ],
        compiler_params=_compiler_params(("arbitrary", "arbitrary")),
        name="rnn_scan",
    )(rest, rest, hist8, h0, wc, bc, wra, bra, wri, bri, lam)


def _rnn_step_kernel(xr_ref, gr_ref, hist_ref, h0_ref, wc_ref, bc_ref, wra_ref, bra_ref, wri_ref, bri_ref,
                     lam_ref, out_ref, h_ref):
    xc = bc_ref[...] + xr_ref[...] * wc_ref[RNN_CONV_W - 1:RNN_CONV_W, :]
    for tap in range(RNN_CONV_W - 1):
        xc = xc + hist_ref[tap] * wc_ref[tap:tap + 1, :]
    a, u = _rglru_gates(xc, wra_ref, bra_ref, wri_ref, bri_ref, lam_ref)
    h = a * h0_ref[...] + u
    h_ref[...] = h
    out_ref[...] = (jax.nn.gelu(gr_ref[...]) * h).astype(out_ref.dtype)


def _rnn_step(rest, hist_t, h0, wc, bc, wra, bra, wri, bri, lam):
    bsz = rest.shape[0]
    d = D_MODEL
    vec = pl.BlockSpec((1, d), lambda i: (0, 0))
    blk = pl.BlockSpec((N_RNN_BLOCKS, RNN_BLOCK, RNN_BLOCK), lambda i: (0, 0, 0))
    return pl.pallas_call(
        _rnn_step_kernel,
        grid=(1,),
        in_specs=[
            pl.BlockSpec((bsz, d), lambda i: (0, 0)),
            pl.BlockSpec((bsz, d), lambda i: (0, 1)),
            pl.BlockSpec((RNN_CONV_W - 1, bsz, d), lambda i: (0, 0, 0)),
            pl.BlockSpec((bsz, d), lambda i: (0, 0)),
            pl.BlockSpec((RNN_CONV_W, d), lambda i: (0, 0)),
            vec, blk, vec, blk, vec, vec,
        ],
        out_specs=[pl.BlockSpec((bsz, d), lambda i: (0, 0)), pl.BlockSpec((bsz, d), lambda i: (0, 0))],
        out_shape=[jax.ShapeDtypeStruct((bsz, d), BF16), jax.ShapeDtypeStruct((bsz, d), F32)],
        compiler_params=_compiler_params(("arbitrary",)),
        name="rnn_step",
    )(rest, rest, hist_t, h0, wc, bc, wra, bra, wri, bri, lam)


def _merge_kernel(x_ref, att_ref, rnn_ref, ga_ref, gb_ref, wpa_ref, wpb_ref, wo_ref, g_ref, b_ref, o_ref):
    pa = jnp.dot(att_ref[...], wpa_ref[...], preferred_element_type=F32)
    pb = jnp.dot(rnn_ref[...], wpb_ref[...], preferred_element_type=F32)
    merged = _sigmoid(ga_ref[...]) * pa + _sigmoid(gb_ref[...]) * pb
    y = jnp.dot(merged.astype(BF16), wo_ref[...], preferred_element_type=F32)
    o_ref[...] = _layer_norm(DEEPNORM_ALPHA * x_ref[...] + y, g_ref[...], b_ref[...])


def _merge(x, att, rnn, rest, wpa, wpb, wo, g, b, tm):
    t = x.shape[0]
    d = D_MODEL
    row = pl.BlockSpec((tm, d), lambda i: (i, 0))
    mat = pl.BlockSpec((d, d), lambda i: (0, 0))
    vec = pl.BlockSpec((1, d), lambda i: (0, 0))
    return pl.pallas_call(
        _merge_kernel,
        grid=(t // tm,),
        in_specs=[row, row, row,
                  pl.BlockSpec((tm, d), lambda i: (i, 2)), pl.BlockSpec((tm, d), lambda i: (i, 3)),
                  mat, mat, mat, vec, vec],
        out_specs=row,
        out_shape=jax.ShapeDtypeStruct((t, d), F32),
        compiler_params=_compiler_params(("arbitrary",)),
        name="merge",
    )(x, att, rnn, rest, rest, wpa, wpb, wo, g, b)


def _ffn_kernel(*refs, tm, tf, rows_per_seq, per_row_history):
    if per_row_history:
        (x1_ref, wu_ref, wg_ref, wd_ref, wf_ref, bf_ref, g2_ref, b2_ref, pe_ref, wple_ref, wpleg_ref, bpleg_ref,
         g3_ref, b3_ref, hist_ref, o_ref, gate_ref, acc_ref, xb_ref, up_ref, hid_ref, gbuf_ref) = refs
    else:
        (x1_ref, wu_ref, wg_ref, wd_ref, wf_ref, bf_ref, g2_ref, b2_ref, pe_ref, wple_ref, wpleg_ref, bpleg_ref,
         g3_ref, b3_ref, o_ref, gate_ref, acc_ref, xb_ref, up_ref, hid_ref, gbuf_ref, gtail_ref) = refs
    i = pl.program_id(0)
    j = pl.program_id(1)

    @pl.when(j == 0)
    def _():
        xb_ref[...] = x1_ref[...].astype(BF16)
        acc_ref[...] = jnp.zeros_like(acc_ref)

    if not per_row_history:
        first_of_seq = (i % (rows_per_seq // tm)) == 0

        @pl.when(first_of_seq)
        def _():
            gbuf_ref[0:SUBLANES, :] = jnp.zeros((SUBLANES, tf), F32)

        @pl.when(jnp.logical_not(first_of_seq))
        def _():
            gbuf_ref[0:SUBLANES, :] = gtail_ref[j]

    n_parts = max(1, tm // FFN_PART_ROWS)
    part = tm // n_parts
    rc = min(part, FFN_ROW_CHUNK)
    goff = 0 if per_row_history else SUBLANES

    def project(r0):
        xb = xb_ref[r0:r0 + part, :]
        up_ref[r0:r0 + part, :] = jnp.dot(xb, wu_ref[...], preferred_element_type=F32)
        gbuf_ref[goff + r0:goff + r0 + part, :] = jnp.dot(xb, wg_ref[...], preferred_element_type=F32)

    def gate_chain(r0):
        for c0 in range(r0, r0 + part, rc):
            if per_row_history:
                gate_c = bf_ref[...] + gbuf_ref[c0:c0 + rc, :] * wf_ref[FFN_CONV_W - 1:FFN_CONV_W, :]
                for tap in range(FFN_CONV_W - 1):
                    gate_c = gate_c + hist_ref[tap, c0:c0 + rc, :] * wf_ref[tap:tap + 1, :]
            else:
                window = gbuf_ref[c0:c0 + rc + SUBLANES, :]
                gate_c = bf_ref[...]
                for back in range(FFN_CONV_W):
                    tap = FFN_CONV_W - 1 - back
                    gate_c = gate_c + window[SUBLANES - back:SUBLANES - back + rc] * wf_ref[tap:tap + 1, :]
            hid_ref[c0:c0 + rc, :] = (jax.nn.gelu(gate_c) * up_ref[c0:c0 + rc, :]).astype(BF16)

    def down(r0):
        acc_ref[r0:r0 + part, :] += jnp.dot(hid_ref[r0:r0 + part, :], wd_ref[...], preferred_element_type=F32)

    project(0)
    for k in range(n_parts):
        if k + 1 < n_parts:
            project((k + 1) * part)
        gate_chain(k * part)
        down(k * part)

    if per_row_history:
        gate_ref[...] = gbuf_ref[...]
    else:
        tail = gbuf_ref[tm:tm + SUBLANES, :]
        gtail_ref[j] = tail
        gate_ref[...] = tail

    @pl.when(j == pl.num_programs(1) - 1)
    def _():
        x2 = _layer_norm(DEEPNORM_ALPHA * x1_ref[...] + acc_ref[...], g2_ref[...], b2_ref[...])
        emb = jnp.dot(pe_ref[...].astype(BF16), wple_ref[...], preferred_element_type=F32)
        gate_e = _sigmoid(jnp.dot(x2.astype(BF16), wpleg_ref[...], preferred_element_type=F32) + bpleg_ref[...])
        o_ref[...] = _layer_norm(DEEPNORM_ALPHA * x2 + emb * gate_e, g3_ref[...], b3_ref[...])


def _ffn(x1, pe, wup, wdown, wf, bfc, g2, b2, wple, wpleg, bpleg, g3, b3, tm, tf, rows_per_seq, hist_t=None):
    t = x1.shape[0]
    d = D_MODEL
    n_chunks = D_FF // tf
    per_row_history = hist_t is not None
    row = pl.BlockSpec((tm, d), lambda i, j: (i, 0))
    vec = pl.BlockSpec((1, d), lambda i, j: (0, 0))
    fvec = pl.BlockSpec((1, tf), lambda i, j: (0, j))
    in_specs = [
        row,
        pl.BlockSpec((d, tf), lambda i, j: (0, j)),
        pl.BlockSpec((d, tf), lambda i, j: (0, n_chunks + j)),
        pl.BlockSpec((tf, d), lambda i, j: (j, 0)),
        pl.BlockSpec((FFN_CONV_W, tf), lambda i, j: (0, j)),
        fvec, vec, vec,
        pl.BlockSpec((tm, D_PLE), lambda i, j: (i, 0)),
        pl.BlockSpec((D_PLE, d), lambda i, j: (0, 0)),
        pl.BlockSpec((d, d), lambda i, j: (0, 0)),
        vec, vec, vec,
    ]
    args = [x1, wup, wup, wdown, wf, bfc, g2, b2, pe, wple, wpleg, bpleg, g3, b3]
    scratch = [pltpu.VMEM((tm, d), F32), pltpu.VMEM((tm, d), BF16), pltpu.VMEM((tm, tf), F32), pltpu.VMEM((tm, tf), BF16)]
    if per_row_history:
        scratch.append(pltpu.VMEM((tm, tf), F32))
        in_specs.append(pl.BlockSpec((FFN_CONV_W - 1, tm, tf), lambda i, j: (0, i, j)))
        args.append(hist_t)
        gate_spec = pl.BlockSpec((tm, tf), lambda i, j: (i, j))
        gate_shape = jax.ShapeDtypeStruct((t, D_FF), F32)
    else:
        gate_spec = pl.BlockSpec((None, SUBLANES, tf), lambda i, j: (i, 0, j))
        gate_shape = jax.ShapeDtypeStruct((t // tm, SUBLANES, D_FF), F32)
        scratch += [pltpu.VMEM((tm + SUBLANES, tf), F32), pltpu.VMEM((n_chunks, SUBLANES, tf), F32)]
    kernel = functools.partial(_ffn_kernel, tm=tm, tf=tf, rows_per_seq=rows_per_seq, per_row_history=per_row_history)
    return pl.pallas_call(
        kernel,
        grid=(t // tm, n_chunks),
        in_specs=in_specs,
        out_specs=[row, gate_spec],
        out_shape=[jax.ShapeDtypeStruct((t, d), F32), gate_shape],
        scratch_shapes=scratch,
        compiler_params=_compiler_params(("arbitrary", "arbitrary")),
        name="ffn",
    )(*args)


def kernel(x_prompt, x_sample, p_prompt, p_sample, cache_k, cache_v, page_table, state_rnn_conv, state_rnn_h, state_ffn_conv, w_in, sb_bias, w_rconv, b_rconv, w_ra, b_ra, w_ri, b_ri, lru_lambda, w_pa, w_pb, w_o, ln1_g, ln1_b, w_up, w_fconv, b_fconv, w_down, ln2_g, ln2_b, w_ple, w_pleg, b_pleg, ln3_g, ln3_b):
    assert w_in.shape[0] == DEPTH == 1
    bsz, seq, d = x_prompt.shape
    dec = x_sample.shape[0]
    assert x_sample.shape[1] == 1 and d == D_MODEL
    n_phys = cache_k.shape[1]

    row_vec = lambda a: a.reshape(1, -1)
    w_in_bf = w_in[0].astype(BF16)
    wra, wri = w_ra[0].astype(BF16), w_ri[0].astype(BF16)
    wpa, wpb, wo = w_pa[0].astype(BF16), w_pb[0].astype(BF16), w_o[0].astype(BF16)
    wup, wdown = w_up[0].astype(BF16), w_down[0].astype(BF16)
    wple, wpleg = w_ple[0].astype(BF16), w_pleg[0].astype(BF16)
    rnn_w = (w_rconv[0], row_vec(b_rconv[0]), wra, row_vec(b_ra[0]), wri, row_vec(b_ri[0]), row_vec(lru_lambda[0]))
    ffn_w = (wup, wdown, w_fconv[0], row_vec(b_fconv[0]), row_vec(ln2_g[0]), row_vec(ln2_b[0]),
             wple, wpleg, row_vec(b_pleg[0]), row_vec(ln3_g[0]), row_vec(ln3_b[0]))
    merge_w = (wpa, wpb, wo, row_vec(ln1_g[0]), row_vec(ln1_b[0]))

    xp = x_prompt.reshape(bsz * seq, d)
    xs = x_sample.reshape(dec, d)
    wt_kv_bf = jnp.transpose(w_in[0][:, d:3 * d]).astype(BF16)
    q_p, kt_p, vt_p, kb_p, vb_p, rest_p = _inproj(xp.astype(BF16), w_in_bf, tm=512, wt_kv_bf=wt_kv_bf, rows_per_seq=seq)
    q_s, k_s, v_s, rest_s = _inproj(xs.astype(BF16), w_in_bf, tm=dec)

    bias2 = sb_bias[0] * LOG2E
    k_pool = jnp.transpose(cache_k[0], (0, 2, 3, 1)).reshape(n_phys, d, PAGE_SIZE)
    v_pool = jnp.transpose(cache_v[0], (0, 2, 3, 1)).reshape(n_phys, d, PAGE_SIZE)
    att_p, att_s = _sb_attention(q_p, kb_p, vb_p, bias2, q_s.reshape(dec, 1, d), k_s.reshape(dec, 1, d),
                                 v_s.reshape(dec, 1, d), k_pool, v_pool, page_table, tq=256, pairs_per_step=4)
    att_s = att_s.reshape(dec, d)

    rnn_p, h_p = _rnn_scan(rest_p, jnp.zeros((bsz, SUBLANES, d), F32), jnp.zeros((bsz, 1, d), F32), *rnn_w,
                           bsz=bsz, seq=seq, ts=256)
    x1_p = _merge(xp, att_p, rnn_p, rest_p, *merge_w, tm=512)
    ffn_tm = 512
    y_p, gate_tails = _ffn(x1_p, p_prompt[0].reshape(bsz * seq, D_PLE), *ffn_w, tm=ffn_tm, tf=512, rows_per_seq=seq)
    rest_p4 = rest_p.reshape(bsz, seq, 4 * d)
    rconv_p = rest_p4[:, seq - (RNN_CONV_W - 1):, :d]
    fconv_p = gate_tails.reshape(bsz, seq // ffn_tm, SUBLANES, D_FF)[:, -1, SUBLANES - (FFN_CONV_W - 1):, :]
    hd = (N_HEADS, HEAD_DIM)
    k_p = jnp.transpose(kt_p.reshape(bsz, *hd, seq), (0, 3, 1, 2))
    v_p = jnp.transpose(vt_p.reshape(bsz, *hd, seq), (0, 3, 1, 2))

    rnn_s, h_s = _rnn_step(rest_s, jnp.swapaxes(state_rnn_conv[0], 0, 1), state_rnn_h[0], *rnn_w)
    x1_s = _merge(xs, att_s, rnn_s, rest_s, *merge_w, tm=dec)
    y_s, gate_s = _ffn(x1_s, p_sample[0].reshape(dec, D_PLE), *ffn_w, tm=dec, tf=512, rows_per_seq=1,
                       hist_t=jnp.swapaxes(state_ffn_conv[0], 0, 1))
    rconv_s = jnp.concatenate([state_rnn_conv[0][:, 1:], rest_s[:, None, :d]], axis=1)
    fconv_s = jnp.concatenate([state_ffn_conv[0][:, 1:], gate_s[:, None, :]], axis=1)

    return (y_p.reshape(bsz, seq, d), y_s.reshape(dec, 1, d),
            k_p[None], v_p[None],
            rconv_p[None], h_p.reshape(1, bsz, d), fconv_p[None],
            k_s.reshape(1, dec, 1, *hd), v_s.reshape(1, dec, 1, *hd),
            rconv_s[None], h_s[None], fconv_s[None])
```
